```python
import functools
import jax, jax.numpy as jnp
from jax import lax
import numpy as np


D_MODEL = 1024
BATCH = 8
SEQ = 2048
DEPTH = 2
DEC_BATCH = 128
DEC_SEQ = 8
PAST_LEN = 16384
PAGE_SIZE = 128

H_MLA = 8
Q_LORA = 384
KV_LORA = 256
NOPE = 64
ROPE = 32
V_MLA = 64
MLA_SCALE = (NOPE + ROPE) ** -0.5
H_RET = 8
DK_RET = 64
DV_RET = 64
RET_CHUNK = 128
FFN_DIM = 2816
CONV_W = 3
Q_BLOCK = 128
ROPE_THETA = 10000.0
EPS = 1e-6
GN_EPS = 1e-5
NEG_INF = -1e30

kernel_name = 'mla_retention_gated_convffn_decode_step'


def rmsnorm(x, g):
    xf = x.astype(jnp.float32)
    y = xf * lax.rsqrt(jnp.mean(xf * xf, axis=-1, keepdims=True) + EPS) * g.astype(jnp.float32)
    return y.astype(x.dtype)


def rope(x, pos):
    d = x.shape[-1]
    half = d // 2
    inv = jnp.power(ROPE_THETA, -jnp.arange(half, dtype=jnp.float32) * (2.0 / d))
    ang = pos.astype(jnp.float32)[:, None] * inv[None, :]
    cos = jnp.cos(ang)[None, :, None, :]
    sin = jnp.sin(ang)[None, :, None, :]
    xf = x.astype(jnp.float32)
    x1, x2 = xf[..., :half], xf[..., half:]
    return jnp.concatenate([x1 * cos - x2 * sin, x1 * sin + x2 * cos], axis=-1).astype(x.dtype)


def retention_log_decay():
    return jnp.log1p(-jnp.exp2(-5.0 - jnp.arange(H_RET, dtype=jnp.float32)))


def split_cols(z):
    sizes = (Q_LORA, KV_LORA, ROPE, H_RET * DK_RET, H_RET * DK_RET,
             H_RET * DV_RET, H_RET * DV_RET, 2 * D_MODEL)
    outs = []
    off = 0
    for s in sizes:
        outs.append(z[..., off:off + s])
        off += s
    return outs


def mla_prompt(q_lat, q_pe, ckv, kpe):
    B, S, H, L = q_lat.shape
    nb = S // Q_BLOCK
    key_pos = jnp.arange(S)

    def blocks(a):
        return jnp.swapaxes(a.reshape(B, nb, Q_BLOCK, *a.shape[2:]), 0, 1)

    def one_block(args):
        ql, qp, i = args
        s = (jnp.einsum('bqhl,bkl->bhqk', ql, ckv)
             + jnp.einsum('bqhr,bkr->bhqk', qp, kpe)).astype(jnp.float32) * MLA_SCALE
        q_pos = i * Q_BLOCK + jnp.arange(Q_BLOCK)
        s = jnp.where(key_pos[None, :] <= q_pos[:, None], s, NEG_INF)
        w = jax.nn.softmax(s, axis=-1).astype(ckv.dtype)
        return jnp.einsum('bhqk,bkl->bqhl', w, ckv)

    o = lax.map(one_block, (blocks(q_lat), blocks(q_pe), jnp.arange(nb)))
    return jnp.swapaxes(o, 0, 1).reshape(B, S, H, L)


def mla_sample(q_lat, q_pe, ckv_new, kpe_new, cache_ckv_l, cache_kpe_l, page_table):
    T = q_lat.shape[1]
    causal = jnp.tril(jnp.ones((T, T), dtype=bool))

    def one_seq(args):
        ql, qp, cn, kn, pt = args
        past_c = cache_ckv_l[pt].reshape(-1, KV_LORA).astype(cn.dtype)
        past_k = cache_kpe_l[pt].reshape(-1, ROPE).astype(kn.dtype)
        s_past = (jnp.einsum('qhl,kl->hqk', ql, past_c)
                  + jnp.einsum('qhr,kr->hqk', qp, past_k)).astype(jnp.float32) * MLA_SCALE
        s_new = (jnp.einsum('qhl,kl->hqk', ql, cn)
                 + jnp.einsum('qhr,kr->hqk', qp, kn)).astype(jnp.float32) * MLA_SCALE
        s_new = jnp.where(causal[None], s_new, NEG_INF)
        w = jax.nn.softmax(jnp.concatenate([s_past, s_new], axis=-1), axis=-1).astype(cn.dtype)
        n_past = past_c.shape[0]
        return (jnp.einsum('hqk,kl->qhl', w[..., :n_past], past_c)
                + jnp.einsum('hqk,kl->qhl', w[..., n_past:], cn))

    return lax.map(one_seq, (q_lat, q_pe, ckv_new, kpe_new, page_table))


def retention_chunk(s_prev, q, k, v, log_g):
    q = q.astype(jnp.float32)
    k = k.astype(jnp.float32)
    v = v.astype(jnp.float32)
    s_prev = s_prev.astype(jnp.float32)
    C = q.shape[1]
    idx = jnp.arange(C, dtype=jnp.float32)
    diff = idx[:, None] - idx[None, :]
    decay = jnp.where(diff[None] >= 0,
                      jnp.exp(jnp.maximum(diff, 0.0)[None] * log_g[:, None, None]), 0.0)
    scores = jnp.einsum('bihd,bjhd->bhij', q, k) * decay[None]
    inner = jnp.einsum('bhij,bjhv->bihv', scores, v)
    q_dec = jnp.exp((idx + 1.0)[:, None] * log_g[None, :])[None, :, :, None]
    cross = jnp.einsum('bihd,bhdv->bihv', q, s_prev) * q_dec
    k_dec = jnp.exp((C - 1.0 - idx)[:, None] * log_g[None, :])[None, :, :, None]
    s_new = (jnp.exp(C * log_g)[None, :, None, None] * s_prev
             + jnp.einsum('bjhd,bjhv->bhdv', k * k_dec, v))
    return inner + cross, s_new


def retention_prompt(q, k, v, log_g):
    B, S = q.shape[:2]
    nc = S // RET_CHUNK

    def chunks(a):
        return jnp.swapaxes(a.reshape(B, nc, RET_CHUNK, *a.shape[2:]), 0, 1)

    def step(s, xs):
        qc, kc, vc = xs
        o, s_new = retention_chunk(s, qc, kc, vc, log_g)
        return s_new, o

    s0 = jnp.zeros((B, H_RET, DK_RET, DV_RET), jnp.float32)
    s_final, o = lax.scan(step, s0, (chunks(q), chunks(k), chunks(v)))
    return jnp.swapaxes(o, 0, 1).reshape(B, S, H_RET, DV_RET), s_final


def retention_step(q, k, v, s_prev, log_g):
    return retention_chunk(s_prev, q, k, v, log_g)


def head_groupnorm(o):
    mu = jnp.mean(o, axis=-1, keepdims=True)
    var = jnp.mean(jnp.square(o - mu), axis=-1, keepdims=True)
    return (o - mu) * lax.rsqrt(var + GN_EPS)


def conv_ffn(xn, buf, w_up, conv_w, conv_b, w_down):
    up = xn @ w_up
    a, b = up[..., :FFN_DIM], up[..., FFN_DIM:]
    T = a.shape[1]
    a_ext = jnp.concatenate([buf.astype(a.dtype), a], axis=1)
    conv = conv_b + sum(a_ext[:, j:j + T] * conv_w[j] for j in range(CONV_W))
    h = jax.nn.gelu(conv, approximate=True) * b
    return h @ w_down, a_ext[:, T:]


def trunk_layer(x, pos, conv_buf, attend, retain, p):
    (g_mix_pre, g_mix_post, g_ffn_pre, g_ffn_post, w_in, g_q_a, w_uq, g_kv_a, w_uk, w_uv,
     w_o_mla, w_o_ret, w_out, w_up, conv_w, conv_b, w_down) = p
    B, T, _ = x.shape
    xn = rmsnorm(x, g_mix_pre)
    c_q, c_kv, k_pe, rq, rk, rv, rg, gates = split_cols(xn @ w_in)
    q = (rmsnorm(c_q, g_q_a) @ w_uq).reshape(B, T, H_MLA, NOPE + ROPE)
    q_lat = jnp.einsum('bthn,lhn->bthl', q[..., :NOPE], w_uk)
    q_pe = rope(q[..., NOPE:], pos)
    ckv = rmsnorm(c_kv, g_kv_a)
    kpe = rope(k_pe[:, :, None, :], pos)[:, :, 0, :]
    o_lat = attend(q_lat, q_pe, ckv, kpe)
    o_mla = jnp.einsum('bthl,lhv->bthv', o_lat, w_uv).reshape(B, T, H_MLA * V_MLA)
    rq = rope(rq.reshape(B, T, H_RET, DK_RET), pos)
    rk = rope(rk.reshape(B, T, H_RET, DK_RET), pos) * (DK_RET ** -0.5)
    rv = rv.reshape(B, T, H_RET, DV_RET)
    o_r, s_new = retain(rq, rk, rv)
    o_ret = head_groupnorm(o_r).reshape(B, T, H_RET * DV_RET).astype(x.dtype) * jax.nn.silu(rg)
    g_a = jax.nn.sigmoid(gates[..., :D_MODEL])
    g_b = jax.nn.sigmoid(gates[..., D_MODEL:])
    mix = (g_a * (o_mla @ w_o_mla) + g_b * (o_ret @ w_o_ret)) @ w_out
    x = x + rmsnorm(mix, g_mix_post)
    f, conv_new = conv_ffn(rmsnorm(x, g_ffn_pre), conv_buf, w_up, conv_w, conv_b, w_down)
    x = x + rmsnorm(f, g_ffn_post)
    return x, ckv, kpe, s_new, conv_new


def setup_inputs(seed: int = 0) -> dict:
    key = jax.random.key(seed)
    ks = jax.random.split(key, 32)
    f32 = jnp.float32
    n_pages = PAST_LEN // PAGE_SIZE
    n_used = DEC_BATCH * n_pages
    n_pool = (n_used * 5) // 4
    in_cols = Q_LORA + KV_LORA + ROPE + 2 * H_RET * DK_RET + 2 * H_RET * DV_RET + 2 * D_MODEL

    def nrm(k, shape, scale):
        return jax.random.normal(k, shape, f32) * scale

    def gain(k, n):
        return 1.0 + 0.01 * jax.random.normal(k, (DEPTH, n), f32)

    page_table = jax.random.permutation(ks[6], n_pool)[:n_used].reshape(DEC_BATCH, n_pages).astype(jnp.int32)
    return {
        'x_prompt': nrm(ks[0], (BATCH, SEQ, D_MODEL), 1.0),
        'x_sample': nrm(ks[1], (DEC_BATCH, DEC_SEQ, D_MODEL), 1.0),
        'cache_ckv': nrm(ks[2], (DEPTH, n_pool, PAGE_SIZE, KV_LORA), 1.0),
        'cache_kpe': nrm(ks[3], (DEPTH, n_pool, PAGE_SIZE, ROPE), 1.0),
        'state_ret': nrm(ks[4], (DEPTH, DEC_BATCH, H_RET, DK_RET, DV_RET), 1.0),
        'state_conv': nrm(ks[5], (DEPTH, DEC_BATCH, CONV_W - 1, FFN_DIM), 1.0),
        'page_table': page_table,
        'norm_mix_pre': gain(ks[7], D_MODEL),
        'norm_mix_post': gain(ks[8], D_MODEL),
        'norm_ffn_pre': gain(ks[9], D_MODEL),
        'norm_ffn_post': gain(ks[10], D_MODEL),
        'w_in': nrm(ks[11], (DEPTH, D_MODEL, in_cols), D_MODEL ** -0.5),
        'norm_q_a': gain(ks[12], Q_LORA),
        'w_uq': nrm(ks[13], (DEPTH, Q_LORA, H_MLA * (NOPE + ROPE)), Q_LORA ** -0.5),
        'norm_kv_a': gain(ks[14], KV_LORA),
        'w_uk': nrm(ks[15], (DEPTH, KV_LORA, H_MLA, NOPE), KV_LORA ** -0.5),
        'w_uv': nrm(ks[16], (DEPTH, KV_LORA, H_MLA, V_MLA), KV_LORA ** -0.5),
        'w_o_mla': nrm(ks[17], (DEPTH, H_MLA * V_MLA, D_MODEL), (H_MLA * V_MLA) ** -0.5),
        'w_o_ret': nrm(ks[18], (DEPTH, H_RET * DV_RET, D_MODEL), (H_RET * DV_RET) ** -0.5),
        'w_out': nrm(ks[19], (DEPTH, D_MODEL, D_MODEL), D_MODEL ** -0.5),
        'w_up': nrm(ks[20], (DEPTH, D_MODEL, 2 * FFN_DIM), D_MODEL ** -0.5),
        'conv_w': nrm(ks[21], (DEPTH, CONV_W, FFN_DIM), CONV_W ** -0.5),
        'conv_b': nrm(ks[22], (DEPTH, FFN_DIM), 0.01),
        'w_down': nrm(ks[23], (DEPTH, FFN_DIM, D_MODEL), FFN_DIM ** -0.5),
    }


def reference(x_prompt, x_sample, cache_ckv, cache_kpe, state_ret, state_conv, page_table,
              norm_mix_pre, norm_mix_post, norm_ffn_pre, norm_ffn_post, w_in, norm_q_a, w_uq,
              norm_kv_a, w_uk, w_uv, w_o_mla, w_o_ret, w_out, w_up, conv_w, conv_b, w_down):
    log_g = retention_log_decay()
    pos_p = jnp.arange(x_prompt.shape[1], dtype=jnp.int32)
    past_len = page_table.shape[1] * PAGE_SIZE
    pos_s = past_len + jnp.arange(x_sample.shape[1], dtype=jnp.int32)
    conv_zero = jnp.zeros((x_prompt.shape[0], CONV_W - 1, FFN_DIM), x_prompt.dtype)
    retain_p = functools.partial(retention_prompt, log_g=log_g)

    xp, xs = x_prompt, x_sample
    ckv_p, kpe_p, ret_p, conv_p = [], [], [], []
    ckv_s, kpe_s, ret_s, conv_s = [], [], [], []
    for l in range(DEPTH):
        p = (norm_mix_pre[l], norm_mix_post[l], norm_ffn_pre[l], norm_ffn_post[l], w_in[l],
             norm_q_a[l], w_uq[l], norm_kv_a[l], w_uk[l], w_uv[l], w_o_mla[l], w_o_ret[l],
             w_out[l], w_up[l], conv_w[l], conv_b[l], w_down[l])
        xp, c1, k1, s1, b1 = trunk_layer(xp, pos_p, conv_zero, mla_prompt, retain_p, p)
        ckv_p.append(c1)
        kpe_p.append(k1)
        ret_p.append(s1)
        conv_p.append(b1)
        attend_s = functools.partial(mla_sample, cache_ckv_l=cache_ckv[l],
                                     cache_kpe_l=cache_kpe[l], page_table=page_table)
        retain_s = functools.partial(retention_step, s_prev=state_ret[l], log_g=log_g)
        xs, c2, k2, s2, b2 = trunk_layer(xs, pos_s, state_conv[l], attend_s, retain_s, p)
        ckv_s.append(c2)
        kpe_s.append(k2)
        ret_s.append(s2)
        conv_s.append(b2)
    return (xp, xs,
            jnp.stack(ckv_p), jnp.stack(kpe_p), jnp.stack(ret_p), jnp.stack(conv_p),
            jnp.stack(ckv_s), jnp.stack(kpe_s), jnp.stack(ret_s), jnp.stack(conv_s))
```

```python
import functools

import jax
import jax.numpy as jnp
from jax import lax
from jax.experimental import pallas as pl
from jax.experimental.pallas import tpu as pltpu

F32 = jnp.float32
BF16 = jnp.bfloat16

D_MODEL = 1024
N_HEADS = 8
Q_LORA = 384
KV_LORA = 256
NOPE = 64
ROPE = 32
V_MLA = 64
DK_RET = 64
DV_RET = 64
FFN_DIM = 2816
CONV_W = 3
PAGE_SIZE = 128
MLA_SCALE = (NOPE + ROPE) ** -0.5
ROPE_THETA = 10000.0
EPS = 1e-6
GN_EPS = 1e-5
NEG_INF = -1e30

LANES = 128
SUBLANES = 8
VMEM_LIMIT_BYTES = 56 * 1024 * 1024

COL_CQ = 0
COL_CKV = COL_CQ + Q_LORA
COL_RQ = COL_CKV + KV_LORA
COL_RK = COL_RQ + N_HEADS * DK_RET
COL_RV = COL_RK + N_HEADS * DK_RET
COL_RG = COL_RV + N_HEADS * DV_RET
COL_GA = COL_RG + N_HEADS * DV_RET
COL_GB = COL_GA + D_MODEL
COL_KPE = COL_GB + D_MODEL
IN_COLS = COL_KPE + LANES
KCAT = KV_LORA + LANES
RET_W = N_HEADS * DK_RET
N_PAIRS = N_HEADS // 2


def _const_spec(shape):
    nd = len(shape)
    return pl.BlockSpec(shape, lambda *_: (0,) * nd, pipeline_mode=pl.Buffered(1))


def _rms(x, g):
    return x * lax.rsqrt(jnp.mean(x * x, axis=-1, keepdims=True) + EPS) * g


def _rope_tile(x, cos, sin_lo, sin_hi, half):
    up = pltpu.roll(x, LANES - half, 1)
    down = pltpu.roll(x, half, 1)
    return x * cos + up * sin_lo + down * sin_hi


def _dot(a, b):
    return jnp.dot(a, b, preferred_element_type=F32)


def _dot_nt(a, b):
    return lax.dot_general(a, b, (((1,), (1,)), ((), ())), preferred_element_type=F32)


def _dot_tn(a, b):
    return lax.dot_general(a, b, (((0,), (0,)), ((), ())), preferred_element_type=F32)


def _in_proj_kernel(x_ref, gpre_ref, win_ref, gq_ref, wuq_ref, gkv_ref, wuk_ref,
                    cr_ref, slr_ref, shr_ref, cm_ref, slm_ref, shm_ref,
                    qcat_ref, kcat_ref, ckv_ref, kpe_ref, rq_ref, rk_ref, rv_ref,
                    rg_ref, ga_ref, gb_ref, xn_scr):
    xn_scr[...] = _rms(x_ref[...], gpre_ref[...]).astype(BF16)

    def proj(lo, width):
        return _dot(xn_scr[...], win_ref[:, lo:lo + width])

    cq = _rms(proj(COL_CQ, Q_LORA), gq_ref[...]).astype(BF16)
    q = _dot(cq, wuq_ref[...]) * MLA_SCALE
    for p in range(N_PAIRS):
        qp = q[:, p * LANES:(p + 1) * LANES].astype(BF16)
        for e in range(2):
            h = 2 * p + e
            qcat_ref[h, :, 0:KV_LORA] = _dot(qp, wuk_ref[h]).astype(BF16)
    lane = lax.broadcasted_iota(jnp.int32, (1, LANES), 1)
    heads_per_tile = LANES // ROPE
    pe_off = N_HEADS * NOPE
    for g in range(N_HEADS // heads_per_tile):
        sl = slice(g * LANES, (g + 1) * LANES)
        blk = _rope_tile(q[:, pe_off + g * LANES:pe_off + (g + 1) * LANES],
                         cm_ref[:, sl], slm_ref[:, sl], shm_ref[:, sl], ROPE // 2)
        for e in range(heads_per_tile):
            h = heads_per_tile * g + e
            keep = (lane >= e * ROPE) & (lane < (e + 1) * ROPE)
            qcat_ref[h, :, KV_LORA:KCAT] = jnp.where(keep, blk, 0.0).astype(BF16)

    ckv = _rms(proj(COL_CKV, KV_LORA), gkv_ref[...])
    ckv_ref[...] = ckv
    kcat_ref[:, 0:KV_LORA] = ckv.astype(BF16)
    kp = _rope_tile(proj(COL_KPE, LANES), cm_ref[:, 0:LANES], slm_ref[:, 0:LANES],
                    shm_ref[:, 0:LANES], ROPE // 2)
    kcat_ref[:, KV_LORA:KCAT] = kp.astype(BF16)
    kpe_ref[...] = kp[:, 0:ROPE]

    for j in range(RET_W // LANES):
        sl = slice(j * LANES, (j + 1) * LANES)
        rq = _rope_tile(proj(COL_RQ + j * LANES, LANES), cr_ref[:, sl], slr_ref[:, sl],
                        shr_ref[:, sl], DK_RET // 2)
        rq_ref[:, sl] = rq.astype(rq_ref.dtype)
        rk = _rope_tile(proj(COL_RK + j * LANES, LANES), cr_ref[:, sl], slr_ref[:, sl],
                        shr_ref[:, sl], DK_RET // 2) * (DK_RET ** -0.5)
        rk_ref[:, sl] = rk.astype(rk_ref.dtype)
    rv_ref[...] = proj(COL_RV, RET_W).astype(rv_ref.dtype)
    rg = proj(COL_RG, RET_W)
    rg_ref[...] = (rg * jax.nn.sigmoid(rg)).astype(rg_ref.dtype)
    ga_ref[...] = jax.nn.sigmoid(proj(COL_GA, D_MODEL)).astype(ga_ref.dtype)
    gb_ref[...] = jax.nn.sigmoid(proj(COL_GB, D_MODEL)).astype(gb_ref.dtype)


def _in_proj(x2d, tabs, tab_blocks, w, tq):
    n = x2d.shape[0]
    grid = (n // tq,)
    row = lambda width: pl.BlockSpec((tq, width), lambda i: (i, 0))
    tab = lambda width: pl.BlockSpec((tq, width), lambda i: (i % tab_blocks, 0))
    in_specs = [
        row(D_MODEL), _const_spec((1, D_MODEL)), _const_spec((D_MODEL, IN_COLS)),
        _const_spec((1, Q_LORA)), _const_spec((Q_LORA, N_HEADS * (NOPE + ROPE))),
        _const_spec((1, KV_LORA)), _const_spec((N_HEADS, LANES, KV_LORA)),
        tab(RET_W), tab(RET_W), tab(RET_W),
        tab(N_HEADS * ROPE), tab(N_HEADS * ROPE), tab(N_HEADS * ROPE),
    ]
    out_shape = (
        jax.ShapeDtypeStruct((N_HEADS, n, KCAT), BF16),
        jax.ShapeDtypeStruct((n, KCAT), BF16),
        jax.ShapeDtypeStruct((n, KV_LORA), F32),
        jax.ShapeDtypeStruct((n, ROPE), F32),
        jax.ShapeDtypeStruct((n, RET_W), BF16),
        jax.ShapeDtypeStruct((n, RET_W), BF16),
        jax.ShapeDtypeStruct((n, RET_W), BF16),
        jax.ShapeDtypeStruct((n, RET_W), BF16),
        jax.ShapeDtypeStruct((n, D_MODEL), BF16),
        jax.ShapeDtypeStruct((n, D_MODEL), BF16),
    )
    out_specs = (
        pl.BlockSpec((N_HEADS, tq, KCAT), lambda i: (0, i, 0)),
        row(KCAT), row(KV_LORA), row(ROPE), row(RET_W), row(RET_W), row(RET_W), row(RET_W),
        row(D_MODEL), row(D_MODEL),
    )
    return pl.pallas_call(
        _in_proj_kernel, grid=grid, in_specs=in_specs, out_specs=out_specs, out_shape=out_shape,
        scratch_shapes=[pltpu.VMEM((tq, D_MODEL), BF16)],
        compiler_params=pltpu.CompilerParams(dimension_semantics=("parallel",),
                                             vmem_limit_bytes=VMEM_LIMIT_BYTES),
        name="in_proj",
    )(x2d, w["g_mix_pre"], w["w_in"], w["g_q"], w["w_uq"], w["g_kv"], w["w_uk"], *tabs)


def _softmax_update(s, v, m_ref, l_ref, acc_ref):
    m_old = m_ref[...]
    m_new = jnp.maximum(m_old, jnp.max(s, axis=-1, keepdims=True))
    alpha = jnp.exp(m_old - m_new)
    p = jnp.exp(s - m_new)
    l_ref[...] = alpha * l_ref[...] + jnp.sum(p, axis=-1, keepdims=True)
    acc_ref[...] = alpha * acc_ref[...] + _dot(p.astype(BF16), v)
    m_ref[...] = m_new


def _value_proj_pairs(o_of_head, wuv_ref, write):
    for p in range(N_PAIRS):
        write(p, _dot(o_of_head(2 * p), wuv_ref[2 * p]) + _dot(o_of_head(2 * p + 1), wuv_ref[2 * p + 1]))


def _attn_prompt_kernel(qi_ref, ki_ref, q_ref, k_ref, wuv_ref, o_ref, m_scr, l_scr, acc_scr):
    step = pl.program_id(1)
    qi = qi_ref[step]
    ki = ki_ref[step]
    tq = q_ref.shape[1]

    @pl.when(ki == 0)
    def _():
        m_scr[...] = jnp.full(m_scr.shape, NEG_INF, F32)
        l_scr[...] = jnp.zeros(l_scr.shape, F32)
        acc_scr[...] = jnp.zeros(acc_scr.shape, F32)

    def update(masked):
        k = k_ref[...]
        v = k[:, 0:KV_LORA]
        if masked:
            r = lax.broadcasted_iota(jnp.int32, (tq, tq), 0)
            c = lax.broadcasted_iota(jnp.int32, (tq, tq), 1)
            causal = c <= r
        for h in range(N_HEADS):
            s = _dot_nt(q_ref[h], k)
            if masked:
                s = jnp.where(causal, s, NEG_INF)
            _softmax_update(s, v, m_scr.at[h], l_scr.at[h], acc_scr.at[h])

    @pl.when(ki < qi)
    def _():
        update(False)

    @pl.when(ki == qi)
    def _():
        update(True)

        def o_of_head(h):
            return (acc_scr[h] * (1.0 / l_scr[h])).astype(BF16)

        def write(p, val):
            o_ref[:, p * LANES:(p + 1) * LANES] = val.astype(o_ref.dtype)

        _value_proj_pairs(o_of_head, wuv_ref, write)


def _attn_prompt(qcat, kcat, wuv, batch, seq, tq):
    nq = seq // tq
    pairs = [(qi, ki) for qi in range(nq) for ki in range(qi + 1)]
    qi_tab = jnp.asarray([p[0] for p in pairs], jnp.int32)
    ki_tab = jnp.asarray([p[1] for p in pairs], jnp.int32)
    n = batch * seq
    grid_spec = pltpu.PrefetchScalarGridSpec(
        num_scalar_prefetch=2,
        grid=(batch, len(pairs)),
        in_specs=[
            pl.BlockSpec((N_HEADS, tq, KCAT), lambda b, s, qt, kt: (0, b * nq + qt[s], 0)),
            pl.BlockSpec((tq, KCAT), lambda b, s, qt, kt: (b * nq + kt[s], 0)),
            pl.BlockSpec((N_HEADS, KV_LORA, LANES), lambda b, s, qt, kt: (0, 0, 0),
                         pipeline_mode=pl.Buffered(1)),
        ],
        out_specs=pl.BlockSpec((tq, N_HEADS * V_MLA), lambda b, s, qt, kt: (b * nq + qt[s], 0)),
        scratch_shapes=[pltpu.VMEM((N_HEADS, tq, 1), F32), pltpu.VMEM((N_HEADS, tq, 1), F32),
                        pltpu.VMEM((N_HEADS, tq, KV_LORA), F32)],
    )
    return pl.pallas_call(
        _attn_prompt_kernel, grid_spec=grid_spec,
        out_shape=jax.ShapeDtypeStruct((n, N_HEADS * V_MLA), BF16),
        compiler_params=pltpu.CompilerParams(dimension_semantics=("parallel", "arbitrary"),
                                             vmem_limit_bytes=VMEM_LIMIT_BYTES),
        name="attn_prompt",
    )(qi_tab, ki_tab, qcat, kcat, wuv)


def _attn_sample_kernel(pt_ref, ql_ref, qp_ref, kn_c_ref, kn_p_ref, wuv_ref, *rest, pages_per_step):
    G = pages_per_step
    ckv_refs = rest[:G]
    kpe_refs = rest[G:2 * G]
    o_ref = rest[2 * G]
    m_scr, l_scr, acc_scr = rest[2 * G + 1:]
    g = pl.program_id(1)
    ql = ql_ref[0]
    qp = qp_ref[0]
    rows = ql.shape[0]
    t_new = rows // N_HEADS

    @pl.when(g == 0)
    def _():
        kc = kn_c_ref[0]
        kp = kn_p_ref[0]
        s = _dot_nt(ql, kc) + _dot_nt(qp, kp)
        r = lax.broadcasted_iota(jnp.int32, s.shape, 0) % t_new
        c = lax.broadcasted_iota(jnp.int32, s.shape, 1)
        s = jnp.where(c <= r, s, NEG_INF)
        m = jnp.max(s, axis=-1, keepdims=True)
        p = jnp.exp(s - m)
        m_scr[...] = m
        l_scr[...] = jnp.sum(p, axis=-1, keepdims=True)
        acc_scr[...] = _dot(p.astype(BF16), kc)

    kc = jnp.concatenate([r[...].astype(BF16) for r in ckv_refs], axis=0)
    kp = jnp.concatenate([r[...].astype(BF16) for r in kpe_refs], axis=0)
    s = _dot_nt(ql, kc) + _dot_nt(qp, kp)
    _softmax_update(s, kc, m_scr, l_scr, acc_scr)

    @pl.when(g == pl.num_programs(1) - 1)
    def _():
        o = (acc_scr[...] * (1.0 / l_scr[...])).astype(BF16)

        for p in range(N_PAIRS):
            even = _dot(o, wuv_ref[2 * p])[(2 * p) * t_new:(2 * p + 1) * t_new]
            odd = _dot(o, wuv_ref[2 * p + 1])[(2 * p + 1) * t_new:(2 * p + 2) * t_new]
            o_ref[0, :, p * LANES:(p + 1) * LANES] = even + odd


def _attn_sample(page_table, q_lat, q_pe, kn_c, kn_p, wuv, cache_ckv, cache_kpe, layer, pages_per_step):
    bdec, rows, _ = q_lat.shape
    n_pages = page_table.shape[1]
    G = pages_per_step
    t_new = rows // N_HEADS
    pt_flat = page_table.reshape(-1)

    def page_spec(width, j):
        return pl.BlockSpec((None, None, PAGE_SIZE, width),
                            lambda b, g, pt: (layer, pt[b * n_pages + g * G + j], 0, 0))

    per_seq = lambda r, width: pl.BlockSpec((1, r, width), lambda b, g, pt: (b, 0, 0))
    grid_spec = pltpu.PrefetchScalarGridSpec(
        num_scalar_prefetch=1,
        grid=(bdec, n_pages // G),
        in_specs=[per_seq(rows, KV_LORA), per_seq(rows, ROPE),
                  per_seq(PAGE_SIZE, KV_LORA), per_seq(PAGE_SIZE, ROPE),
                  pl.BlockSpec((N_HEADS, KV_LORA, LANES), lambda b, g, pt: (0, 0, 0),
                               pipeline_mode=pl.Buffered(1))]
                 + [page_spec(KV_LORA, j) for j in range(G)]
                 + [page_spec(ROPE, j) for j in range(G)],
        out_specs=pl.BlockSpec((1, t_new, N_HEADS * V_MLA), lambda b, g, pt: (b, 0, 0)),
        scratch_shapes=[pltpu.VMEM((rows, 1), F32), pltpu.VMEM((rows, 1), F32),
                        pltpu.VMEM((rows, KV_LORA), F32)],
    )
    return pl.pallas_call(
        functools.partial(_attn_sample_kernel, pages_per_step=G), grid_spec=grid_spec,
        out_shape=jax.ShapeDtypeStruct((bdec, t_new, N_HEADS * V_MLA), F32),
        compiler_params=pltpu.CompilerParams(dimension_semantics=("parallel", "arbitrary"),
                                             vmem_limit_bytes=VMEM_LIMIT_BYTES),
        name="attn_sample",
    )(pt_flat, q_lat, q_pe, kn_c, kn_p, wuv, *([cache_ckv] * G), *([cache_kpe] * G))


def _ret_intra_pair(q, k, v, d_even, d_odd, left):
    zero = jnp.zeros_like(q)
    s0 = _dot_nt(jnp.where(left, q, zero), k)
    s1 = _dot_nt(jnp.where(left, zero, q), k)
    p0 = (s0 * d_even).astype(BF16)
    p1 = (s1 * d_odd).astype(BF16)
    return _dot(p0, jnp.where(left, v, zero)) + _dot(p1, jnp.where(left, zero, v))


def _group_norm_gate(o, avg_ref, gate):
    mu = _dot(o.astype(BF16), avg_ref[...])
    d = o - mu
    var = _dot((d * d).astype(BF16), avg_ref[...])
    return d * lax.rsqrt(var + GN_EPS) * gate


def _ret_prompt_kernel(q_ref, k_ref, v_ref, gate_ref, dec_ref, qd_ref, kd_ref, gc_ref, avg_ref,
                       o_ref, s_out_ref, s_scr):
    c = pl.program_id(1)

    @pl.when(c == 0)
    def _():
        s_scr[...] = jnp.zeros(s_scr.shape, F32)

    lane = lax.broadcasted_iota(jnp.int32, (1, LANES), 1)
    left = lane < DK_RET
    rr = lax.broadcasted_iota(jnp.int32, (LANES, LANES), 0)
    cc = lax.broadcasted_iota(jnp.int32, (LANES, LANES), 1)
    same_head = (rr < DK_RET) == (cc < DV_RET)
    outs = []
    for p in range(N_PAIRS):
        sl = slice(p * LANES, (p + 1) * LANES)
        q = q_ref[:, sl]
        k = k_ref[:, sl]
        v = v_ref[:, sl]
        inner = _ret_intra_pair(q, k, v, dec_ref[2 * p], dec_ref[2 * p + 1], left)
        state = s_scr[p]
        cross = _dot(q, state.astype(BF16)) * qd_ref[p]
        outs.append(inner + cross)
        kd = (k.astype(F32) * kd_ref[p]).astype(BF16)
        upd = _dot_tn(kd, v)
        s_scr[p] = gc_ref[p] * state + jnp.where(same_head, upd, 0.0)
    o = jnp.concatenate(outs, axis=-1)
    o_ref[...] = _group_norm_gate(o, avg_ref, gate_ref[...].astype(F32)).astype(o_ref.dtype)

    @pl.when(c == pl.num_programs(1) - 1)
    def _():
        s_out_ref[0] = s_scr[...]


def _ret_prompt(rq, rk, rv, gate, tabs, avg, batch, seq, chunk):
    nc = seq // chunk
    n = batch * seq
    row = pl.BlockSpec((chunk, RET_W), lambda b, c: (b * nc + c, 0))
    dec, qd, kd, gc = tabs
    return pl.pallas_call(
        _ret_prompt_kernel, grid=(batch, nc),
        in_specs=[row, row, row, row, _const_spec(dec.shape), _const_spec(qd.shape),
                  _const_spec(kd.shape), _const_spec(gc.shape), _const_spec(avg.shape)],
        out_specs=(row, pl.BlockSpec((1, N_PAIRS, LANES, LANES), lambda b, c: (b, 0, 0, 0))),
        out_shape=(jax.ShapeDtypeStruct((n, RET_W), BF16),
                   jax.ShapeDtypeStruct((batch, N_PAIRS, LANES, LANES), F32)),
        scratch_shapes=[pltpu.VMEM((N_PAIRS, LANES, LANES), F32)],
        compiler_params=pltpu.CompilerParams(dimension_semantics=("parallel", "arbitrary"),
                                             vmem_limit_bytes=VMEM_LIMIT_BYTES),
        name="ret_prompt",
    )(rq, rk, rv, gate, dec, qd, kd, gc, avg)


def _ret_sample_kernel(q_ref, k_ref, v_ref, gate_ref, s_ref, dec_ref, qd_ref, kd_ref, gc_ref,
                       seqmask_ref, avg_ref, o_ref, s_out_ref):
    nseq = s_ref.shape[0]
    lane = lax.broadcasted_iota(jnp.int32, (1, LANES), 1)
    left = lane < DK_RET
    seqmask = seqmask_ref[...]
    reps = nseq * DK_RET // LANES
    outs = []
    for p in range(N_PAIRS):
        sl = slice(p * LANES, (p + 1) * LANES)
        q = q_ref[:, sl]
        k = k_ref[:, sl]
        v = v_ref[:, sl]
        inner = _ret_intra_pair(q, k, v, dec_ref[2 * p], dec_ref[2 * p + 1], left)
        qf = q.astype(F32)
        kf = k.astype(F32) * kd_ref[p]
        q_sw = pltpu.roll(qf, DK_RET, 1)
        k_sw = pltpu.roll(kf, DK_RET, 1)
        cross = []
        for e in range(2):
            h = 2 * p + e
            keep = left if e == 0 else jnp.logical_not(left)
            qh = jnp.where(keep, qf, q_sw)
            kh = jnp.where(keep, kf, k_sw)
            q_exp = (jnp.concatenate([qh] * reps, axis=-1) * seqmask).astype(BF16)
            k_exp = (jnp.concatenate([kh] * reps, axis=-1) * seqmask).astype(BF16)
            state = s_ref[:, h].reshape(nseq * DK_RET, DV_RET)
            cross.append(_dot(q_exp, state.astype(BF16)) * qd_ref[h])
            vh = v[:, e * DV_RET:(e + 1) * DV_RET]
            upd = _dot_tn(k_exp, vh)
            s_out_ref[:, h] = (gc_ref[h] * state + upd).reshape(nseq, DK_RET, DV_RET)
        outs.append(inner + jnp.concatenate(cross, axis=-1))
    o = jnp.concatenate(outs, axis=-1)
    o_ref[...] = _group_norm_gate(o, avg_ref, gate_ref[...].astype(F32)).astype(o_ref.dtype)


def _ret_sample(rq, rk, rv, gate, state, layer, tabs, seqmask, avg, nseq, t_new):
    n = rq.shape[0]
    rows = nseq * t_new
    bdec = n // t_new
    row = pl.BlockSpec((rows, RET_W), lambda i: (i, 0))
    st_in = pl.BlockSpec((None, nseq, N_HEADS, DK_RET, DV_RET), lambda i: (layer, i, 0, 0, 0))
    st_out = pl.BlockSpec((nseq, N_HEADS, DK_RET, DV_RET), lambda i: (i, 0, 0, 0))
    dec, qd, kd, gc = tabs
    return pl.pallas_call(
        _ret_sample_kernel, grid=(bdec // nseq,),
        in_specs=[row, row, row, row, st_in, _const_spec(dec.shape), _const_spec(qd.shape),
                  _const_spec(kd.shape), _const_spec(gc.shape), _const_spec(seqmask.shape),
                  _const_spec(avg.shape)],
        out_specs=(row, st_out),
        out_shape=(jax.ShapeDtypeStruct((n, RET_W), BF16),
                   jax.ShapeDtypeStruct((bdec, N_HEADS, DK_RET, DV_RET), F32)),
        compiler_params=pltpu.CompilerParams(dimension_semantics=("parallel",),
                                             vmem_limit_bytes=VMEM_LIMIT_BYTES),
        name="ret_sample",
    )(rq, rk, rv, gate, state, dec, qd, kd, gc, seqmask, avg)


FFN_CHUNK = 256


def _merge_ffn_kernel(*refs, long_seq, tiles_per_seq, seq_rows):
    if long_seq:
        (x_ref, om_ref, orr_ref, ga_ref, gb_ref, wom_ref, wor_ref, wout_ref, gpost_ref, gfpre_ref,
         wup_ref, cw_ref, cb_ref, wdn_ref, gfpost_ref, y_ref, cnew_ref, a_scr, h_scr, xn_scr) = refs
    else:
        (x_ref, om_ref, orr_ref, ga_ref, gb_ref, wom_ref, wor_ref, wout_ref, gpost_ref, gfpre_ref,
         wup_ref, cw_ref, cb_ref, wdn_ref, gfpost_ref, e1_ref, e2_ref, y_ref, a_ref,
         a_scr, h_scr, xn_scr) = refs
    tq = x_ref.shape[0]
    a1 = _dot(om_ref[...].astype(BF16), wom_ref[...])
    a2 = _dot(orr_ref[...].astype(BF16), wor_ref[...])
    mixed = (ga_ref[...].astype(F32) * a1 + gb_ref[...].astype(F32) * a2).astype(BF16)
    x1 = x_ref[...] + _rms(_dot(mixed, wout_ref[...]), gpost_ref[...])
    xn_scr[...] = _rms(x1, gfpre_ref[...]).astype(BF16)

    @pl.when(pl.program_id(0) % tiles_per_seq == 0)
    def _():
        a_scr[0:SUBLANES, :] = jnp.zeros((SUBLANES, FFN_DIM), F32)

    if not long_seq:
        row_in_seq = lax.broadcasted_iota(jnp.int32, (tq, 1), 0) % seq_rows

    for c0 in range(0, FFN_DIM, FFN_CHUNK):
        sl = slice(c0, c0 + FFN_CHUNK)
        a = _dot(xn_scr[...], wup_ref[:, c0:c0 + FFN_CHUNK])
        b = _dot(xn_scr[...], wup_ref[:, FFN_DIM + c0:FFN_DIM + c0 + FFN_CHUNK])
        a_scr[SUBLANES:SUBLANES + tq, sl] = a
        a_m1 = a_scr[SUBLANES - 1:SUBLANES - 1 + tq, sl]
        a_m2 = a_scr[SUBLANES - 2:SUBLANES - 2 + tq, sl]
        if not long_seq:
            a_m1 = jnp.where(row_in_seq == 0, e1_ref[:, sl], a_m1)
            a_m2 = jnp.where(row_in_seq < 2, e2_ref[:, sl], a_m2)
            a_ref[:, sl] = a
        conv = cb_ref[:, sl] + a_m2 * cw_ref[0:1, sl] + a_m1 * cw_ref[1:2, sl] + a * cw_ref[2:3, sl]
        h_scr[:, sl] = (jax.nn.gelu(conv, approximate=True) * b).astype(BF16)
    f = _dot(h_scr[...], wdn_ref[...])
    y_ref[...] = x1 + _rms(f, gfpost_ref[...])
    if long_seq:
        cnew_ref[0] = a_scr[tq + SUBLANES - (CONV_W - 1):tq + SUBLANES, :]
        a_scr[0:SUBLANES, :] = a_scr[tq:tq + SUBLANES, :]


def _merge_ffn(x2d, o_mla, o_ret, ga, gb, w, tq, seq, bounds=None):
    n = x2d.shape[0]
    long_seq = bounds is None
    assert (seq % tq == 0) if long_seq else (tq % seq == 0)
    row = lambda width: pl.BlockSpec((tq, width), lambda i: (i, 0))
    in_specs = [row(D_MODEL), row(N_HEADS * V_MLA), row(RET_W), row(D_MODEL), row(D_MODEL),
                _const_spec((N_HEADS * V_MLA, D_MODEL)), _const_spec((RET_W, D_MODEL)),
                _const_spec((D_MODEL, D_MODEL)), _const_spec((1, D_MODEL)), _const_spec((1, D_MODEL)),
                _const_spec((D_MODEL, 2 * FFN_DIM)), _const_spec((CONV_W, FFN_DIM)),
                _const_spec((1, FFN_DIM)), _const_spec((FFN_DIM, D_MODEL)), _const_spec((1, D_MODEL))]
    args = [x2d, o_mla, o_ret, ga, gb, w["w_o_mla"], w["w_o_ret"], w["w_out"], w["g_mix_post"],
            w["g_ffn_pre"], w["w_up"], w["conv_w"], w["conv_b"], w["w_down"], w["g_ffn_post"]]
    if long_seq:
        tiles_per_seq = seq // tq
        out_shape = (jax.ShapeDtypeStruct((n, D_MODEL), F32),
                     jax.ShapeDtypeStruct((n // seq, CONV_W - 1, FFN_DIM), F32))
        out_specs = (row(D_MODEL),
                     pl.BlockSpec((1, CONV_W - 1, FFN_DIM), lambda i: (i // tiles_per_seq, 0, 0)))
    else:
        tiles_per_seq = n // tq
        in_specs += [row(FFN_DIM), row(FFN_DIM)]
        args += list(bounds)
        out_shape = (jax.ShapeDtypeStruct((n, D_MODEL), F32), jax.ShapeDtypeStruct((n, FFN_DIM), F32))
        out_specs = (row(D_MODEL), row(FFN_DIM))
    return pl.pallas_call(
        functools.partial(_merge_ffn_kernel, long_seq=long_seq, tiles_per_seq=tiles_per_seq,
                          seq_rows=seq),
        grid=(n // tq,), in_specs=in_specs, out_specs=out_specs, out_shape=out_shape,
        scratch_shapes=[pltpu.VMEM((tq + SUBLANES, FFN_DIM), F32), pltpu.VMEM((tq, FFN_DIM), BF16),
                        pltpu.VMEM((tq, D_MODEL), BF16)],
        compiler_params=pltpu.CompilerParams(dimension_semantics=("arbitrary",),
                                             vmem_limit_bytes=VMEM_LIMIT_BYTES),
        name="merge_ffn",
    )(*args)


def _rope_tables(pos, d, reps):
    half = d // 2
    inv = jnp.power(ROPE_THETA, -jnp.arange(half, dtype=F32) * (2.0 / d))
    ang = pos.astype(F32)[:, None] * inv[None, :]
    cos, sin = jnp.cos(ang), jnp.sin(ang)
    zero = jnp.zeros_like(sin)
    rep = lambda a: jnp.tile(a, (1, reps))
    return (rep(jnp.concatenate([cos, cos], -1)), rep(jnp.concatenate([-sin, zero], -1)),
            rep(jnp.concatenate([zero, sin], -1)))


def _log_decay():
    return jnp.log1p(-jnp.exp2(-5.0 - jnp.arange(N_HEADS, dtype=F32)))


def _pair_lanes(per_head):
    c = per_head.shape[1]
    x = jnp.broadcast_to(per_head[:, :, None], (N_HEADS, c, DK_RET))
    return x.reshape(N_PAIRS, 2, c, DK_RET).transpose(0, 2, 1, 3).reshape(N_PAIRS, c, LANES)


def _ret_tables_prompt(chunk):
    log_g = _log_decay()
    idx = jnp.arange(chunk, dtype=F32)
    diff = idx[:, None] - idx[None, :]
    dec = jnp.where(diff[None] >= 0, jnp.exp(jnp.maximum(diff, 0.0)[None] * log_g[:, None, None]), 0.0)
    qd = _pair_lanes(jnp.exp((idx + 1.0)[None, :] * log_g[:, None]))
    kd = _pair_lanes(jnp.exp((chunk - 1.0 - idx)[None, :] * log_g[:, None]))
    gc = _pair_lanes(jnp.exp(chunk * log_g)[:, None])
    return dec, qd, kd, gc


def _ret_tables_sample(nseq, t_new):
    log_g = _log_decay()
    rows = nseq * t_new
    r = jnp.arange(rows)
    t = (r % t_new).astype(F32)
    same = (r[:, None] // t_new) == (r[None, :] // t_new)
    diff = t[:, None] - t[None, :]
    dec = jnp.where((same & (diff >= 0))[None],
                    jnp.exp(jnp.maximum(diff, 0.0)[None] * log_g[:, None, None]), 0.0)
    qd = jnp.exp((t + 1.0)[None, :] * log_g[:, None])[:, :, None]
    kd = _pair_lanes(jnp.exp((t_new - 1.0 - t)[None, :] * log_g[:, None]))
    gc = jnp.exp(t_new * log_g)[:, None, None]
    seqmask = ((r[:, None] // t_new) == (jnp.arange(nseq * DK_RET)[None, :] // DK_RET)).astype(F32)
    return (dec, qd, kd, gc), seqmask


def _group_avg():
    g = jnp.arange(RET_W) // DV_RET
    return jnp.where(g[:, None] == g[None, :], 1.0 / DV_RET, 0.0).astype(BF16)


def _prep_weights(norm_mix_pre, norm_mix_post, norm_ffn_pre, norm_ffn_post, w_in, norm_q_a, w_uq,
                  norm_kv_a, w_uk, w_uv, w_o_mla, w_o_ret, w_out, w_up, conv_w, conv_b, w_down):
    depth = w_in.shape[0]
    k0 = Q_LORA + KV_LORA
    w_in_r = jnp.concatenate([w_in[:, :, :k0], w_in[:, :, k0 + ROPE:],
                              jnp.tile(w_in[:, :, k0:k0 + ROPE], (1, 1, LANES // ROPE))], axis=-1)
    uq = w_uq.reshape(depth, Q_LORA, N_HEADS, NOPE + ROPE)
    w_uq_r = jnp.concatenate([uq[..., :NOPE].reshape(depth, Q_LORA, N_HEADS * NOPE),
                              uq[..., NOPE:].reshape(depth, Q_LORA, N_HEADS * ROPE)], axis=-1)
    uk = jnp.transpose(w_uk, (0, 2, 3, 1))
    odd = (jnp.arange(N_HEADS) % 2 == 1)[None, :, None, None]
    zeros = jnp.zeros_like(uk)
    w_uk_r = jnp.concatenate([jnp.where(odd, zeros, uk), jnp.where(odd, uk, zeros)], axis=2)
    uv = jnp.transpose(w_uv, (0, 2, 1, 3))
    zeros = jnp.zeros_like(uv)
    w_uv_r = jnp.concatenate([jnp.where(odd, zeros, uv), jnp.where(odd, uv, zeros)], axis=3)
    vec = lambda g: g[:, None, :]
    return dict(
        g_mix_pre=vec(norm_mix_pre), g_mix_post=vec(norm_mix_post), g_ffn_pre=vec(norm_ffn_pre),
        g_ffn_post=vec(norm_ffn_post), g_q=vec(norm_q_a), g_kv=vec(norm_kv_a),
        w_in=w_in_r.astype(BF16), w_uq=w_uq_r.astype(BF16), w_uk=w_uk_r.astype(BF16),
        w_uv=w_uv_r.astype(BF16), w_o_mla=w_o_mla.astype(BF16), w_o_ret=w_o_ret.astype(BF16),
        w_out=w_out.astype(BF16), w_up=w_up.astype(BF16), conv_w=conv_w, conv_b=vec(conv_b),
        w_down=w_down.astype(BF16))


def _pair_state_to_heads(sp):
    even = sp[:, :, :DK_RET, :DV_RET]
    odd = sp[:, :, DK_RET:, DV_RET:]
    return jnp.stack([even, odd], axis=2).reshape(sp.shape[0], N_HEADS, DK_RET, DV_RET)


def _pick_tile(n, target):
    t = min(n, target)
    while n % t:
        t //= 2
    return t


def kernel(x_prompt, x_sample, cache_ckv, cache_kpe, state_ret, state_conv, page_table, norm_mix_pre, norm_mix_post, norm_ffn_pre, norm_ffn_post, w_in, norm_q_a, w_uq, norm_kv_a, w_uk, w_uv, w_o_mla, w_o_ret, w_out, w_up, conv_w, conv_b, w_down):
    batch, seq, _ = x_prompt.shape
    bdec, t_new, _ = x_sample.shape
    depth = w_in.shape[0]
    n_pages = page_table.shape[1]
    past_len = n_pages * PAGE_SIZE
    n_p, n_s = batch * seq, bdec * t_new

    tq_p = _pick_tile(seq, 256)
    tq_s = _pick_tile(n_s, 256)
    chunk = _pick_tile(seq, 256)
    ret_nseq = _pick_tile(bdec, LANES // t_new)
    pages_per_step = _pick_tile(n_pages, 8)

    weights = _prep_weights(norm_mix_pre, norm_mix_post, norm_ffn_pre, norm_ffn_post, w_in, norm_q_a,
                            w_uq, norm_kv_a, w_uk, w_uv, w_o_mla, w_o_ret, w_out, w_up, conv_w, conv_b,
                            w_down)
    pos_p = jnp.arange(seq, dtype=jnp.int32)
    pos_s = past_len + (jnp.arange(tq_s, dtype=jnp.int32) % t_new)
    tabs_p = _rope_tables(pos_p, DK_RET, N_HEADS) + _rope_tables(pos_p, ROPE, N_HEADS)
    tabs_s = _rope_tables(pos_s, DK_RET, N_HEADS) + _rope_tables(pos_s, ROPE, N_HEADS)
    ret_tabs_p = _ret_tables_prompt(chunk)
    ret_tabs_s, seqmask = _ret_tables_sample(ret_nseq, t_new)
    avg = _group_avg()

    xp = x_prompt.reshape(n_p, D_MODEL)
    xs = x_sample.reshape(n_s, D_MODEL)
    outs = [[] for _ in range(8)]
    for l in range(depth):
        w = {k: v[l] for k, v in weights.items()}

        (qcat, kcat, ckv, kpe, rq, rk, rv, rg, ga, gb) = _in_proj(xp, tabs_p, seq // tq_p, w, tq_p)
        o_mla = _attn_prompt(qcat, kcat, w["w_uv"], batch, seq, tq_p)
        o_ret, s_pair = _ret_prompt(rq, rk, rv, rg, ret_tabs_p, avg, batch, seq, chunk)
        xp, conv_new = _merge_ffn(xp, o_mla, o_ret, ga, gb, w, tq_p, seq)
        outs[0].append(ckv.reshape(batch, seq, KV_LORA))
        outs[1].append(kpe.reshape(batch, seq, ROPE))
        outs[2].append(_pair_state_to_heads(s_pair))
        outs[3].append(conv_new)

        (qcat, kcat, ckv, kpe, rq, rk, rv, rg, ga, gb) = _in_proj(xs, tabs_s, 1, w, tq_s)
        q4 = qcat.reshape(N_HEADS, bdec, t_new, KCAT).transpose(1, 0, 2, 3)
        q_lat = q4[..., :KV_LORA].reshape(bdec, N_HEADS * t_new, KV_LORA)
        q_pe = q4[..., KV_LORA:].astype(F32).reshape(bdec, N_HEADS, t_new, LANES // ROPE, ROPE).sum(axis=3)
        q_pe = q_pe.astype(BF16).reshape(bdec, N_HEADS * t_new, ROPE)
        pad = ((0, 0), (0, PAGE_SIZE - t_new), (0, 0))
        kn_c = jnp.pad(ckv.reshape(bdec, t_new, KV_LORA).astype(BF16), pad)
        kn_p = jnp.pad(kpe.reshape(bdec, t_new, ROPE).astype(BF16), pad)
        o_mla = _attn_sample(page_table, q_lat, q_pe, kn_c, kn_p, w["w_uv"], cache_ckv, cache_kpe, l,
                             pages_per_step).reshape(n_s, N_HEADS * V_MLA)
        o_ret, s_new = _ret_sample(rq, rk, rv, rg, state_ret, l, ret_tabs_s, seqmask, avg, ret_nseq, t_new)
        buf = state_conv[l]
        zrow = jnp.zeros((bdec, t_new - 1, FFN_DIM), F32)
        e1 = jnp.concatenate([buf[:, 1:2], zrow], axis=1).reshape(n_s, FFN_DIM)
        e2 = jnp.concatenate([buf, zrow[:, 1:]], axis=1).reshape(n_s, FFN_DIM)
        xs, a_full = _merge_ffn(xs, o_mla, o_ret, ga, gb, w, tq_s, t_new, bounds=(e1, e2))
        outs[4].append(ckv.reshape(bdec, t_new, KV_LORA))
        outs[5].append(kpe.reshape(bdec, t_new, ROPE))
        outs[6].append(s_new)
        outs[7].append(a_full.reshape(bdec, t_new, FFN_DIM)[:, t_new - (CONV_W - 1):])

    stacked = [jnp.stack(o) for o in outs]
    return (xp.reshape(batch, seq, D_MODEL), xs.reshape(bdec, t_new, D_MODEL), *stacked)
```

```python
import functools

import jax
import jax.numpy as jnp
from jax import lax
from jax.experimental import pallas as pl
from jax.experimental.pallas import tpu as pltpu

F32 = jnp.float32
BF16 = jnp.bfloat16

D_MODEL = 1024
N_HEADS = 8
Q_LORA = 384
KV_LORA = 256
NOPE = 64
ROPE = 32
V_MLA = 64
DK_RET = 64
DV_RET = 64
FFN_DIM = 2816
CONV_W = 3
PAGE_SIZE = 128
MLA_SCALE = (NOPE + ROPE) ** -0.5
LOG2_E = 1.4426950408889634
ROPE_THETA = 10000.0
EPS = 1e-6
GN_EPS = 1e-5
NEG_INF = -1e30

LANES = 128
SUBLANES = 8
VMEM_LIMIT_BYTES = 56 * 1024 * 1024

COL_CQ = 0
COL_CKV = COL_CQ + Q_LORA
COL_RQ = COL_CKV + KV_LORA
COL_RK = COL_RQ + N_HEADS * DK_RET
COL_RV = COL_RK + N_HEADS * DK_RET
COL_RG = COL_RV + N_HEADS * DV_RET
COL_GA = COL_RG + N_HEADS * DV_RET
COL_GB = COL_GA + D_MODEL
COL_KPE = COL_GB + D_MODEL
IN_COLS = COL_KPE + LANES
KCAT = KV_LORA + LANES
RET_W = N_HEADS * DK_RET
N_PAIRS = N_HEADS // 2


def _const_spec(shape):
    nd = len(shape)
    return pl.BlockSpec(shape, lambda *_: (0,) * nd, pipeline_mode=pl.Buffered(1))


def _rms(x, g):
    return x * lax.rsqrt(jnp.mean(x * x, axis=-1, keepdims=True) + EPS) * g


def _rope_tile(x, cos, sin_lo, sin_hi, half):
    up = pltpu.roll(x, LANES - half, 1)
    down = pltpu.roll(x, half, 1)
    return x * cos + up * sin_lo + down * sin_hi


def _dot(a, b):
    return jnp.dot(a, b, preferred_element_type=F32)


def _dot_nt(a, b):
    return lax.dot_general(a, b, (((1,), (1,)), ((), ())), preferred_element_type=F32)


def _dot_tn(a, b):
    return lax.dot_general(a, b, (((0,), (0,)), ((), ())), preferred_element_type=F32)


def _in_proj_kernel(x_ref, gpre_ref, win_ref, gq_ref, wuq_ref, gkv_ref, wuk_ref,
                    cr_ref, slr_ref, shr_ref, cm_ref, slm_ref, shm_ref,
                    qcat_ref, kcat_ref, ckv_ref, kpe_ref, rq_ref, rk_ref, rv_ref,
                    rg_ref, ga_ref, gb_ref, xn_scr):
    xn_scr[...] = _rms(x_ref[...], gpre_ref[...]).astype(BF16)

    def proj(lo, width):
        return _dot(xn_scr[...], win_ref[:, lo:lo + width])

    cq = _rms(proj(COL_CQ, Q_LORA), gq_ref[...]).astype(BF16)
    q = _dot(cq, wuq_ref[...]) * (MLA_SCALE * LOG2_E)
    for p in range(N_PAIRS):
        qp = q[:, p * LANES:(p + 1) * LANES].astype(BF16)
        for e in range(2):
            h = 2 * p + e
            qcat_ref[h, :, 0:KV_LORA] = _dot(qp, wuk_ref[h]).astype(BF16)
    lane = lax.broadcasted_iota(jnp.int32, (1, LANES), 1)
    heads_per_tile = LANES // ROPE
    pe_off = N_HEADS * NOPE
    for g in range(N_HEADS // heads_per_tile):
        sl = slice(g * LANES, (g + 1) * LANES)
        blk = _rope_tile(q[:, pe_off + g * LANES:pe_off + (g + 1) * LANES],
                         cm_ref[:, sl], slm_ref[:, sl], shm_ref[:, sl], ROPE // 2)
        for e in range(heads_per_tile):
            h = heads_per_tile * g + e
            keep = (lane >= e * ROPE) & (lane < (e + 1) * ROPE)
            qcat_ref[h, :, KV_LORA:KCAT] = jnp.where(keep, blk, 0.0).astype(BF16)

    ckv = _rms(proj(COL_CKV, KV_LORA), gkv_ref[...])
    ckv_ref[...] = ckv
    kcat_ref[:, 0:KV_LORA] = ckv.astype(BF16)
    kp = _rope_tile(proj(COL_KPE, LANES), cm_ref[:, 0:LANES], slm_ref[:, 0:LANES],
                    shm_ref[:, 0:LANES], ROPE // 2)
    kcat_ref[:, KV_LORA:KCAT] = kp.astype(BF16)
    kpe_ref[...] = kp[:, 0:ROPE]

    for j in range(RET_W // LANES):
        sl = slice(j * LANES, (j + 1) * LANES)
        rq = _rope_tile(proj(COL_RQ + j * LANES, LANES), cr_ref[:, sl], slr_ref[:, sl],
                        shr_ref[:, sl], DK_RET // 2)
        rq_ref[:, sl] = rq.astype(rq_ref.dtype)
        rk = _rope_tile(proj(COL_RK + j * LANES, LANES), cr_ref[:, sl], slr_ref[:, sl],
                        shr_ref[:, sl], DK_RET // 2) * (DK_RET ** -0.5)
        rk_ref[:, sl] = rk.astype(rk_ref.dtype)
    rv_ref[...] = proj(COL_RV, RET_W).astype(rv_ref.dtype)
    rg = proj(COL_RG, RET_W)
    rg_ref[...] = (rg * jax.nn.sigmoid(rg)).astype(rg_ref.dtype)
    ga_ref[...] = jax.nn.sigmoid(proj(COL_GA, D_MODEL)).astype(ga_ref.dtype)
    gb_ref[...] = jax.nn.sigmoid(proj(COL_GB, D_MODEL)).astype(gb_ref.dtype)


def _in_proj(x2d, tabs, tab_blocks, w, tq):
    n = x2d.shape[0]
    grid = (n // tq,)
    row = lambda width: pl.BlockSpec((tq, width), lambda i: (i, 0))
    tab = lambda width: pl.BlockSpec((tq, width), lambda i: (i % tab_blocks, 0))
    in_specs = [
        row(D_MODEL), _const_spec((1, D_MODEL)), _const_spec((D_MODEL, IN_COLS)),
        _const_spec((1, Q_LORA)), _const_spec((Q_LORA, N_HEADS * (NOPE + ROPE))),
        _const_spec((1, KV_LORA)), _const_spec((N_HEADS, LANES, KV_LORA)),
        tab(RET_W), tab(RET_W), tab(RET_W),
        tab(N_HEADS * ROPE), tab(N_HEADS * ROPE), tab(N_HEADS * ROPE),
    ]
    out_shape = (
        jax.ShapeDtypeStruct((N_HEADS, n, KCAT), BF16),
        jax.ShapeDtypeStruct((n, KCAT), BF16),
        jax.ShapeDtypeStruct((n, KV_LORA), F32),
        jax.ShapeDtypeStruct((n, ROPE), F32),
        jax.ShapeDtypeStruct((n, RET_W), BF16),
        jax.ShapeDtypeStruct((n, RET_W), BF16),
        jax.ShapeDtypeStruct((n, RET_W), BF16),
        jax.ShapeDtypeStruct((n, RET_W), BF16),
        jax.ShapeDtypeStruct((n, D_MODEL), BF16),
        jax.ShapeDtypeStruct((n, D_MODEL), BF16),
    )
    out_specs = (
        pl.BlockSpec((N_HEADS, tq, KCAT), lambda i: (0, i, 0)),
        row(KCAT), row(KV_LORA), row(ROPE), row(RET_W), row(RET_W), row(RET_W), row(RET_W),
        row(D_MODEL), row(D_MODEL),
    )
    return pl.pallas_call(
        _in_proj_kernel, grid=grid, in_specs=in_specs, out_specs=out_specs, out_shape=out_shape,
        scratch_shapes=[pltpu.VMEM((tq, D_MODEL), BF16)],
        compiler_params=pltpu.CompilerParams(dimension_semantics=("parallel",),
                                             vmem_limit_bytes=VMEM_LIMIT_BYTES),
        name="in_proj",
    )(x2d, w["g_mix_pre"], w["w_in"], w["g_q"], w["w_uq"], w["g_kv"], w["w_uk"], *tabs)


def _softmax_update(s, v, m_ref, l_ref, acc_ref):
    m_old = m_ref[...]
    m_new = jnp.maximum(m_old, jnp.max(s, axis=-1, keepdims=True))
    alpha = jnp.exp2(m_old - m_new)
    p = jnp.exp2(s - m_new)
    l_ref[...] = alpha * l_ref[...] + jnp.sum(p, axis=-1, keepdims=True)
    acc_ref[...] = alpha * acc_ref[...] + _dot(p.astype(BF16), v)
    m_ref[...] = m_new


def _value_proj_pairs(o_of_head, wuv_ref, write):
    for p in range(N_PAIRS):
        write(p, _dot(o_of_head(2 * p), wuv_ref[2 * p]) + _dot(o_of_head(2 * p + 1), wuv_ref[2 * p + 1]))


def _attn_prompt_kernel(qi_ref, ki_ref, q_ref, k_ref, wuv_ref, o_ref, m_scr, l_scr, acc_scr):
    step = pl.program_id(1)
    qi = qi_ref[step]
    ki = ki_ref[step]
    tq = q_ref.shape[1]

    @pl.when(ki == 0)
    def _():
        m_scr[...] = jnp.full(m_scr.shape, NEG_INF, F32)
        l_scr[...] = jnp.zeros(l_scr.shape, F32)
        acc_scr[...] = jnp.zeros(acc_scr.shape, F32)

    def update(masked):
        k = k_ref[...]
        v = k[:, 0:KV_LORA]
        if masked:
            r = lax.broadcasted_iota(jnp.int32, (tq, tq), 0)
            c = lax.broadcasted_iota(jnp.int32, (tq, tq), 1)
            causal = c <= r
        for h in range(N_HEADS):
            s = _dot_nt(q_ref[h], k)
            if masked:
                s = jnp.where(causal, s, NEG_INF)
            _softmax_update(s, v, m_scr.at[h], l_scr.at[h], acc_scr.at[h])

    @pl.when(ki < qi)
    def _():
        update(False)

    @pl.when(ki == qi)
    def _():
        update(True)

        def o_of_head(h):
            return (acc_scr[h] * (1.0 / l_scr[h])).astype(BF16)

        def write(p, val):
            o_ref[:, p * LANES:(p + 1) * LANES] = val.astype(o_ref.dtype)

        _value_proj_pairs(o_of_head, wuv_ref, write)


def _attn_prompt(qcat, kcat, wuv, batch, seq, tq):
    nq = seq // tq
    pairs = [(qi, ki) for qi in range(nq) for ki in range(qi + 1)]
    qi_tab = jnp.asarray([p[0] for p in pairs], jnp.int32)
    ki_tab = jnp.asarray([p[1] for p in pairs], jnp.int32)
    n = batch * seq
    grid_spec = pltpu.PrefetchScalarGridSpec(
        num_scalar_prefetch=2,
        grid=(batch, len(pairs)),
        in_specs=[
            pl.BlockSpec((N_HEADS, tq, KCAT), lambda b, s, qt, kt: (0, b * nq + qt[s], 0)),
            pl.BlockSpec((tq, KCAT), lambda b, s, qt, kt: (b * nq + kt[s], 0)),
            pl.BlockSpec((N_HEADS, KV_LORA, LANES), lambda b, s, qt, kt: (0, 0, 0),
                         pipeline_mode=pl.Buffered(1)),
        ],
        out_specs=pl.BlockSpec((tq, N_HEADS * V_MLA), lambda b, s, qt, kt: (b * nq + qt[s], 0)),
        scratch_shapes=[pltpu.VMEM((N_HEADS, tq, 1), F32), pltpu.VMEM((N_HEADS, tq, 1), F32),
                        pltpu.VMEM((N_HEADS, tq, KV_LORA), F32)],
    )
    return pl.pallas_call(
        _attn_prompt_kernel, grid_spec=grid_spec,
        out_shape=jax.ShapeDtypeStruct((n, N_HEADS * V_MLA), BF16),
        compiler_params=pltpu.CompilerParams(dimension_semantics=("parallel", "arbitrary"),
                                             vmem_limit_bytes=VMEM_LIMIT_BYTES),
        name="attn_prompt",
    )(qi_tab, ki_tab, qcat, kcat, wuv)


def _attn_sample_kernel(pt_ref, ql_ref, qp_ref, kn_c_ref, kn_p_ref, wuv_ref, ckv_hbm, kpe_hbm, o_ref,
                        ckv_buf, kpe_buf, sem, m_scr, l_scr, acc_scr, *, layer, n_pages, chunk_pages):
    b = pl.program_id(0)
    n_chunks = n_pages // chunk_pages
    ql = ql_ref[0]
    qp = qp_ref[0]
    t_new = ql.shape[0] // N_HEADS

    def chunk_copies(seq, j, slot):
        copies = []
        for i in range(chunk_pages):
            page = pt_ref[seq * n_pages + j * chunk_pages + i]
            copies.append(pltpu.make_async_copy(ckv_hbm.at[layer, page], ckv_buf.at[slot, i], sem.at[slot]))
            copies.append(pltpu.make_async_copy(kpe_hbm.at[layer, page], kpe_buf.at[slot, i], sem.at[slot]))
        return copies

    def start(copies):
        for c in copies:
            c.start()

    @pl.when(b == 0)
    def _():
        start(chunk_copies(b, 0, 0))

    kc = kn_c_ref[0]
    s = _dot_nt(ql, kc) + _dot_nt(qp, kn_p_ref[0])
    r = lax.broadcasted_iota(jnp.int32, s.shape, 0) % t_new
    c = lax.broadcasted_iota(jnp.int32, s.shape, 1)
    s = jnp.where(c <= r, s, NEG_INF)
    m = jnp.max(s, axis=-1, keepdims=True)
    p = jnp.exp2(s - m)
    m_scr[...] = m
    l_scr[...] = jnp.sum(p, axis=-1, keepdims=True)
    acc_scr[...] = _dot(p.astype(BF16), kc)

    for j in range(n_chunks):
        slot = j % 2
        if j + 1 < n_chunks:
            start(chunk_copies(b, j + 1, 1 - slot))
        else:
            @pl.when(b + 1 < pl.num_programs(0))
            def _():
                start(chunk_copies(b + 1, 0, 1 - slot))
        for cp in chunk_copies(b, j, slot):
            cp.wait()
        kc = ckv_buf[slot].reshape(chunk_pages * PAGE_SIZE, KV_LORA).astype(BF16)
        kp = jnp.concatenate([kpe_buf[slot, i] for i in range(chunk_pages)], axis=1).astype(BF16)
        s = _dot_nt(ql, kc) + _dot(qp, kp)
        _softmax_update(s, kc, m_scr, l_scr, acc_scr)

    o = (acc_scr[...] * (1.0 / l_scr[...])).astype(BF16)
    for p in range(N_PAIRS):
        even = _dot(o, wuv_ref[2 * p])[(2 * p) * t_new:(2 * p + 1) * t_new]
        odd = _dot(o, wuv_ref[2 * p + 1])[(2 * p + 1) * t_new:(2 * p + 2) * t_new]
        o_ref[0, :, p * LANES:(p + 1) * LANES] = even + odd


def _attn_sample(page_table, q_lat, q_pe, kn_c, kn_p, wuv, cache_ckv, cache_kpe_t, layer, chunk_pages):
    bdec, rows, _ = q_lat.shape
    n_pages = page_table.shape[1]
    assert n_pages % (2 * chunk_pages) == 0
    t_new = rows // N_HEADS
    pt_flat = page_table.reshape(-1)
    per_seq = lambda r, width: pl.BlockSpec((1, r, width), lambda b, pt: (b, 0, 0))
    grid_spec = pltpu.PrefetchScalarGridSpec(
        num_scalar_prefetch=1,
        grid=(bdec,),
        in_specs=[per_seq(rows, KV_LORA), per_seq(rows, ROPE),
                  per_seq(PAGE_SIZE, KV_LORA), per_seq(PAGE_SIZE, ROPE),
                  pl.BlockSpec((N_HEADS, KV_LORA, LANES), lambda b, pt: (0, 0, 0),
                               pipeline_mode=pl.Buffered(1)),
                  pl.BlockSpec(memory_space=pl.ANY), pl.BlockSpec(memory_space=pl.ANY)],
        out_specs=pl.BlockSpec((1, t_new, N_HEADS * V_MLA), lambda b, pt: (b, 0, 0)),
        scratch_shapes=[pltpu.VMEM((2, chunk_pages, PAGE_SIZE, KV_LORA), F32),
                        pltpu.VMEM((2, chunk_pages, ROPE, PAGE_SIZE), F32),
                        pltpu.SemaphoreType.DMA((2,)),
                        pltpu.VMEM((rows, 1), F32), pltpu.VMEM((rows, 1), F32),
                        pltpu.VMEM((rows, KV_LORA), F32)],
    )
    return pl.pallas_call(
        functools.partial(_attn_sample_kernel, layer=layer, n_pages=n_pages, chunk_pages=chunk_pages),
        grid_spec=grid_spec,
        out_shape=jax.ShapeDtypeStruct((bdec, t_new, N_HEADS * V_MLA), F32),
        compiler_params=pltpu.CompilerParams(dimension_semantics=("arbitrary",),
                                             vmem_limit_bytes=VMEM_LIMIT_BYTES),
        name="attn_sample",
    )(pt_flat, q_lat, q_pe, kn_c, kn_p, wuv, cache_ckv, cache_kpe_t)


def _ret_intra_pair(q, k, v, d_even, d_odd, left):
    zero = jnp.zeros_like(q)
    s0 = _dot_nt(jnp.where(left, q, zero), k)
    s1 = _dot_nt(jnp.where(left, zero, q), k)
    p0 = (s0 * d_even).astype(BF16)
    p1 = (s1 * d_odd).astype(BF16)
    return _dot(p0, jnp.where(left, v, zero)) + _dot(p1, jnp.where(left, zero, v))


def _group_norm_gate(o, avg_ref, gate):
    mu = _dot(o.astype(BF16), avg_ref[...])
    d = o - mu
    var = _dot((d * d).astype(BF16), avg_ref[...])
    return d * lax.rsqrt(var + GN_EPS) * gate


def _ret_prompt_kernel(q_ref, k_ref, v_ref, gate_ref, dec_ref, qd_ref, kd_ref, gc_ref, avg_ref,
                       o_ref, s_out_ref, s_scr):
    c = pl.program_id(1)

    @pl.when(c == 0)
    def _():
        s_scr[...] = jnp.zeros(s_scr.shape, F32)

    lane = lax.broadcasted_iota(jnp.int32, (1, LANES), 1)
    left = lane < DK_RET
    rr = lax.broadcasted_iota(jnp.int32, (LANES, LANES), 0)
    cc = lax.broadcasted_iota(jnp.int32, (LANES, LANES), 1)
    same_head = (rr < DK_RET) == (cc < DV_RET)
    outs = []
    for p in range(N_PAIRS):
        sl = slice(p * LANES, (p + 1) * LANES)
        q = q_ref[:, sl]
        k = k_ref[:, sl]
        v = v_ref[:, sl]
        inner = _ret_intra_pair(q, k, v, dec_ref[2 * p], dec_ref[2 * p + 1], left)
        state = s_scr[p]
        cross = _dot(q, state.astype(BF16)) * qd_ref[p]
        outs.append(inner + cross)
        kd = (k.astype(F32) * kd_ref[p]).astype(BF16)
        upd = _dot_tn(kd, v)
        s_scr[p] = gc_ref[p] * state + jnp.where(same_head, upd, 0.0)
    o = jnp.concatenate(outs, axis=-1)
    o_ref[...] = _group_norm_gate(o, avg_ref, gate_ref[...].astype(F32)).astype(o_ref.dtype)

    @pl.when(c == pl.num_programs(1) - 1)
    def _():
        s_out_ref[0] = s_scr[...]


def _ret_prompt(rq, rk, rv, gate, tabs, avg, batch, seq, chunk):
    nc = seq // chunk
    n = batch * seq
    row = pl.BlockSpec((chunk, RET_W), lambda b, c: (b * nc + c, 0))
    dec, qd, kd, gc = tabs
    return pl.pallas_call(
        _ret_prompt_kernel, grid=(batch, nc),
        in_specs=[row, row, row, row, _const_spec(dec.shape), _const_spec(qd.shape),
                  _const_spec(kd.shape), _const_spec(gc.shape), _const_spec(avg.shape)],
        out_specs=(row, pl.BlockSpec((1, N_PAIRS, LANES, LANES), lambda b, c: (b, 0, 0, 0))),
        out_shape=(jax.ShapeDtypeStruct((n, RET_W), BF16),
                   jax.ShapeDtypeStruct((batch, N_PAIRS, LANES, LANES), F32)),
        scratch_shapes=[pltpu.VMEM((N_PAIRS, LANES, LANES), F32)],
        compiler_params=pltpu.CompilerParams(dimension_semantics=("parallel", "arbitrary"),
                                             vmem_limit_bytes=VMEM_LIMIT_BYTES),
        name="ret_prompt",
    )(rq, rk, rv, gate, dec, qd, kd, gc, avg)


def _ret_sample_kernel(q_ref, k_ref, v_ref, gate_ref, s_ref, dec_ref, qd_ref, kd_ref, gc_ref,
                       seqmask_ref, avg_ref, o_ref, s_out_ref):
    nseq = s_ref.shape[0]
    lane = lax.broadcasted_iota(jnp.int32, (1, LANES), 1)
    left = lane < DK_RET
    seqmask = seqmask_ref[...]
    reps = nseq * DK_RET // LANES
    outs = []
    for p in range(N_PAIRS):
        sl = slice(p * LANES, (p + 1) * LANES)
        q = q_ref[:, sl]
        k = k_ref[:, sl]
        v = v_ref[:, sl]
        inner = _ret_intra_pair(q, k, v, dec_ref[2 * p], dec_ref[2 * p + 1], left)
        qf = q.astype(F32)
        kf = k.astype(F32) * kd_ref[p]
        q_sw = pltpu.roll(qf, DK_RET, 1)
        k_sw = pltpu.roll(kf, DK_RET, 1)
        cross = []
        for e in range(2):
            h = 2 * p + e
            keep = left if e == 0 else jnp.logical_not(left)
            qh = jnp.where(keep, qf, q_sw)
            kh = jnp.where(keep, kf, k_sw)
            q_exp = (jnp.concatenate([qh] * reps, axis=-1) * seqmask).astype(BF16)
            k_exp = (jnp.concatenate([kh] * reps, axis=-1) * seqmask).astype(BF16)
            state = s_ref[:, h].reshape(nseq * DK_RET, DV_RET)
            cross.append(_dot(q_exp, state.astype(BF16)) * qd_ref[h])
            vh = v[:, e * DV_RET:(e + 1) * DV_RET]
            upd = _dot_tn(k_exp, vh)
            s_out_ref[:, h] = (gc_ref[h] * state + upd).reshape(nseq, DK_RET, DV_RET)
        outs.append(inner + jnp.concatenate(cross, axis=-1))
    o = jnp.concatenate(outs, axis=-1)
    o_ref[...] = _group_norm_gate(o, avg_ref, gate_ref[...].astype(F32)).astype(o_ref.dtype)


def _ret_sample(rq, rk, rv, gate, state, layer, tabs, seqmask, avg, nseq, t_new):
    n = rq.shape[0]
    rows = nseq * t_new
    bdec = n // t_new
    row = pl.BlockSpec((rows, RET_W), lambda i: (i, 0))
    st_in = pl.BlockSpec((None, nseq, N_HEADS, DK_RET, DV_RET), lambda i: (layer, i, 0, 0, 0))
    st_out = pl.BlockSpec((nseq, N_HEADS, DK_RET, DV_RET), lambda i: (i, 0, 0, 0))
    dec, qd, kd, gc = tabs
    return pl.pallas_call(
        _ret_sample_kernel, grid=(bdec // nseq,),
        in_specs=[row, row, row, row, st_in, _const_spec(dec.shape), _const_spec(qd.shape),
                  _const_spec(kd.shape), _const_spec(gc.shape), _const_spec(seqmask.shape),
                  _const_spec(avg.shape)],
        out_specs=(row, st_out),
        out_shape=(jax.ShapeDtypeStruct((n, RET_W), BF16),
                   jax.ShapeDtypeStruct((bdec, N_HEADS, DK_RET, DV_RET), F32)),
        compiler_params=pltpu.CompilerParams(dimension_semantics=("parallel",),
                                             vmem_limit_bytes=VMEM_LIMIT_BYTES),
        name="ret_sample",
    )(rq, rk, rv, gate, state, dec, qd, kd, gc, seqmask, avg)


FFN_CHUNK = 256


def _merge_ffn_kernel(*refs, long_seq, tiles_per_seq, seq_rows):
    if long_seq:
        (x_ref, om_ref, orr_ref, ga_ref, gb_ref, wom_ref, wor_ref, wout_ref, gpost_ref, gfpre_ref,
         wup_ref, cw_ref, cb_ref, wdn_ref, gfpost_ref, y_ref, cnew_ref, a_scr, h_scr, xn_scr) = refs
    else:
        (x_ref, om_ref, orr_ref, ga_ref, gb_ref, wom_ref, wor_ref, wout_ref, gpost_ref, gfpre_ref,
         wup_ref, cw_ref, cb_ref, wdn_ref, gfpost_ref, e1_ref, e2_ref, y_ref, a_ref,
         a_scr, h_scr, xn_scr) = refs
    tq = x_ref.shape[0]
    a1 = _dot(om_ref[...].astype(BF16), wom_ref[...])
    a2 = _dot(orr_ref[...].astype(BF16), wor_ref[...])
    mixed = (ga_ref[...].astype(F32) * a1 + gb_ref[...].astype(F32) * a2).astype(BF16)
    x1 = x_ref[...] + _rms(_dot(mixed, wout_ref[...]), gpost_ref[...])
    xn_scr[...] = _rms(x1, gfpre_ref[...]).astype(BF16)

    @pl.when(pl.program_id(0) % tiles_per_seq == 0)
    def _():
        a_scr[0:SUBLANES, :] = jnp.zeros((SUBLANES, FFN_DIM), F32)

    if not long_seq:
        row_in_seq = lax.broadcasted_iota(jnp.int32, (tq, 1), 0) % seq_rows

    for c0 in range(0, FFN_DIM, FFN_CHUNK):
        sl = slice(c0, c0 + FFN_CHUNK)
        a = _dot(xn_scr[...], wup_ref[:, c0:c0 + FFN_CHUNK])
        b = _dot(xn_scr[...], wup_ref[:, FFN_DIM + c0:FFN_DIM + c0 + FFN_CHUNK])
        a_scr[SUBLANES:SUBLANES + tq, sl] = a
        a_m1 = a_scr[SUBLANES - 1:SUBLANES - 1 + tq, sl]
        a_m2 = a_scr[SUBLANES - 2:SUBLANES - 2 + tq, sl]
        if not long_seq:
            a_m1 = jnp.where(row_in_seq == 0, e1_ref[:, sl], a_m1)
            a_m2 = jnp.where(row_in_seq < 2, e2_ref[:, sl], a_m2)
            a_ref[:, sl] = a
        conv = cb_ref[:, sl] + a_m2 * cw_ref[0:1, sl] + a_m1 * cw_ref[1:2, sl] + a * cw_ref[2:3, sl]
        h_scr[:, sl] = (jax.nn.gelu(conv, approximate=True) * b).astype(BF16)
    f = _dot(h_scr[...], wdn_ref[...])
    y_ref[...] = x1 + _rms(f, gfpost_ref[...])
    if long_seq:
        cnew_ref[0] = a_scr[tq + SUBLANES - (CONV_W - 1):tq + SUBLANES, :]
        a_scr[0:SUBLANES, :] = a_scr[tq:tq + SUBLANES, :]


def _merge_ffn(x2d, o_mla, o_ret, ga, gb, w, tq, seq, bounds=None):
    n = x2d.shape[0]
    long_seq = bounds is None
    assert (seq % tq == 0) if long_seq else (tq % seq == 0)
    row = lambda width: pl.BlockSpec((tq, width), lambda i: (i, 0))
    in_specs = [row(D_MODEL), row(N_HEADS * V_MLA), row(RET_W), row(D_MODEL), row(D_MODEL),
                _const_spec((N_HEADS * V_MLA, D_MODEL)), _const_spec((RET_W, D_MODEL)),
                _const_spec((D_MODEL, D_MODEL)), _const_spec((1, D_MODEL)), _const_spec((1, D_MODEL)),
                _const_spec((D_MODEL, 2 * FFN_DIM)), _const_spec((CONV_W, FFN_DIM)),
                _const_spec((1, FFN_DIM)), _const_spec((FFN_DIM, D_MODEL)), _const_spec((1, D_MODEL))]
    args = [x2d, o_mla, o_ret, ga, gb, w["w_o_mla"], w["w_o_ret"], w["w_out"], w["g_mix_post"],
            w["g_ffn_pre"], w["w_up"], w["conv_w"], w["conv_b"], w["w_down"], w["g_ffn_post"]]
    if long_seq:
        tiles_per_seq = seq // tq
        out_shape = (jax.ShapeDtypeStruct((n, D_MODEL), F32),
                     jax.ShapeDtypeStruct((n // seq, CONV_W - 1, FFN_DIM), F32))
        out_specs = (row(D_MODEL),
                     pl.BlockSpec((1, CONV_W - 1, FFN_DIM), lambda i: (i // tiles_per_seq, 0, 0)))
    else:
        tiles_per_seq = n // tq
        in_specs += [row(FFN_DIM), row(FFN_DIM)]
        args += list(bounds)
        out_shape = (jax.ShapeDtypeStruct((n, D_MODEL), F32), jax.ShapeDtypeStruct((n, FFN_DIM), F32))
        out_specs = (row(D_MODEL), row(FFN_DIM))
    return pl.pallas_call(
        functools.partial(_merge_ffn_kernel, long_seq=long_seq, tiles_per_seq=tiles_per_seq,
                          seq_rows=seq),
        grid=(n // tq,), in_specs=in_specs, out_specs=out_specs, out_shape=out_shape,
        scratch_shapes=[pltpu.VMEM((tq + SUBLANES, FFN_DIM), F32), pltpu.VMEM((tq, FFN_DIM), BF16),
                        pltpu.VMEM((tq, D_MODEL), BF16)],
        compiler_params=pltpu.CompilerParams(dimension_semantics=("arbitrary",),
                                             vmem_limit_bytes=VMEM_LIMIT_BYTES),
        name="merge_ffn",
    )(*args)


def _rope_tables(pos, d, reps):
    half = d // 2
    inv = jnp.power(ROPE_THETA, -jnp.arange(half, dtype=F32) * (2.0 / d))
    ang = pos.astype(F32)[:, None] * inv[None, :]
    cos, sin = jnp.cos(ang), jnp.sin(ang)
    zero = jnp.zeros_like(sin)
    rep = lambda a: jnp.tile(a, (1, reps))
    return (rep(jnp.concatenate([cos, cos], -1)), rep(jnp.concatenate([-sin, zero], -1)),
            rep(jnp.concatenate([zero, sin], -1)))


def _log_decay():
    return jnp.log1p(-jnp.exp2(-5.0 - jnp.arange(N_HEADS, dtype=F32)))


def _pair_lanes(per_head):
    c = per_head.shape[1]
    x = jnp.broadcast_to(per_head[:, :, None], (N_HEADS, c, DK_RET))
    return x.reshape(N_PAIRS, 2, c, DK_RET).transpose(0, 2, 1, 3).reshape(N_PAIRS, c, LANES)


def _ret_tables_prompt(chunk):
    log_g = _log_decay()
    idx = jnp.arange(chunk, dtype=F32)
    diff = idx[:, None] - idx[None, :]
    dec = jnp.where(diff[None] >= 0, jnp.exp(jnp.maximum(diff, 0.0)[None] * log_g[:, None, None]), 0.0)
    qd = _pair_lanes(jnp.exp((idx + 1.0)[None, :] * log_g[:, None]))
    kd = _pair_lanes(jnp.exp((chunk - 1.0 - idx)[None, :] * log_g[:, None]))
    gc = _pair_lanes(jnp.exp(chunk * log_g)[:, None])
    return dec, qd, kd, gc


def _ret_tables_sample(nseq, t_new):
    log_g = _log_decay()
    rows = nseq * t_new
    r = jnp.arange(rows)
    t = (r % t_new).astype(F32)
    same = (r[:, None] // t_new) == (r[None, :] // t_new)
    diff = t[:, None] - t[None, :]
    dec = jnp.where((same & (diff >= 0))[None],
                    jnp.exp(jnp.maximum(diff, 0.0)[None] * log_g[:, None, None]), 0.0)
    qd = jnp.exp((t + 1.0)[None, :] * log_g[:, None])[:, :, None]
    kd = _pair_lanes(jnp.exp((t_new - 1.0 - t)[None, :] * log_g[:, None]))
    gc = jnp.exp(t_new * log_g)[:, None, None]
    seqmask = ((r[:, None] // t_new) == (jnp.arange(nseq * DK_RET)[None, :] // DK_RET)).astype(F32)
    return (dec, qd, kd, gc), seqmask


def _group_avg():
    g = jnp.arange(RET_W) // DV_RET
    return jnp.where(g[:, None] == g[None, :], 1.0 / DV_RET, 0.0).astype(BF16)


def _prep_weights(norm_mix_pre, norm_mix_post, norm_ffn_pre, norm_ffn_post, w_in, norm_q_a, w_uq,
                  norm_kv_a, w_uk, w_uv, w_o_mla, w_o_ret, w_out, w_up, conv_w, conv_b, w_down):
    depth = w_in.shape[0]
    k0 = Q_LORA + KV_LORA
    w_in_r = jnp.concatenate([w_in[:, :, :k0], w_in[:, :, k0 + ROPE:],
                              jnp.tile(w_in[:, :, k0:k0 + ROPE], (1, 1, LANES // ROPE))], axis=-1)
    uq = w_uq.reshape(depth, Q_LORA, N_HEADS, NOPE + ROPE)
    w_uq_r = jnp.concatenate([uq[..., :NOPE].reshape(depth, Q_LORA, N_HEADS * NOPE),
                              uq[..., NOPE:].reshape(depth, Q_LORA, N_HEADS * ROPE)], axis=-1)
    uk = jnp.transpose(w_uk, (0, 2, 3, 1))
    odd = (jnp.arange(N_HEADS) % 2 == 1)[None, :, None, None]
    zeros = jnp.zeros_like(uk)
    w_uk_r = jnp.concatenate([jnp.where(odd, zeros, uk), jnp.where(odd, uk, zeros)], axis=2)
    uv = jnp.transpose(w_uv, (0, 2, 1, 3))
    zeros = jnp.zeros_like(uv)
    w_uv_r = jnp.concatenate([jnp.where(odd, zeros, uv), jnp.where(odd, uv, zeros)], axis=3)
    vec = lambda g: g[:, None, :]
    return dict(
        g_mix_pre=vec(norm_mix_pre), g_mix_post=vec(norm_mix_post), g_ffn_pre=vec(norm_ffn_pre),
        g_ffn_post=vec(norm_ffn_post), g_q=vec(norm_q_a), g_kv=vec(norm_kv_a),
        w_in=w_in_r.astype(BF16), w_uq=w_uq_r.astype(BF16), w_uk=w_uk_r.astype(BF16),
        w_uv=w_uv_r.astype(BF16), w_o_mla=w_o_mla.astype(BF16), w_o_ret=w_o_ret.astype(BF16),
        w_out=w_out.astype(BF16), w_up=w_up.astype(BF16), conv_w=conv_w, conv_b=vec(conv_b),
        w_down=w_down.astype(BF16))


def _pair_state_to_heads(sp):
    even = sp[:, :, :DK_RET, :DV_RET]
    odd = sp[:, :, DK_RET:, DV_RET:]
    return jnp.stack([even, odd], axis=2).reshape(sp.shape[0], N_HEADS, DK_RET, DV_RET)


def _pick_tile(n, target):
    t = min(n, target)
    while n % t:
        t //= 2
    return t


def kernel(x_prompt, x_sample, cache_ckv, cache_kpe, state_ret, state_conv, page_table, norm_mix_pre, norm_mix_post, norm_ffn_pre, norm_ffn_post, w_in, norm_q_a, w_uq, norm_kv_a, w_uk, w_uv, w_o_mla, w_o_ret, w_out, w_up, conv_w, conv_b, w_down):
    batch, seq, _ = x_prompt.shape
    bdec, t_new, _ = x_sample.shape
    depth = w_in.shape[0]
    n_pages = page_table.shape[1]
    past_len = n_pages * PAGE_SIZE
    n_p, n_s = batch * seq, bdec * t_new

    tq_p = _pick_tile(seq, 256)
    tq_s = _pick_tile(n_s, 256)
    chunk = _pick_tile(seq, 256)
    ret_nseq = _pick_tile(bdec, LANES // t_new)
    tq_attn = _pick_tile(seq, 512)
    chunk_pages = _pick_tile(n_pages // 2, 32)
    cache_kpe_t = jnp.swapaxes(cache_kpe, 2, 3)

    weights = _prep_weights(norm_mix_pre, norm_mix_post, norm_ffn_pre, norm_ffn_post, w_in, norm_q_a,
                            w_uq, norm_kv_a, w_uk, w_uv, w_o_mla, w_o_ret, w_out, w_up, conv_w, conv_b,
                            w_down)
    pos_p = jnp.arange(seq, dtype=jnp.int32)
    pos_s = past_len + (jnp.arange(tq_s, dtype=jnp.int32) % t_new)
    tabs_p = _rope_tables(pos_p, DK_RET, N_HEADS) + _rope_tables(pos_p, ROPE, N_HEADS)
    tabs_s = _rope_tables(pos_s, DK_RET, N_HEADS) + _rope_tables(pos_s, ROPE, N_HEADS)
    ret_tabs_p = _ret_tables_prompt(chunk)
    ret_tabs_s, seqmask = _ret_tables_sample(ret_nseq, t_new)
    avg = _group_avg()

    xp = x_prompt.reshape(n_p, D_MODEL)
    xs = x_sample.reshape(n_s, D_MODEL)
    outs = [[] for _ in range(8)]
    for l in range(depth):
        w = {k: v[l] for k, v in weights.items()}

        (qcat, kcat, ckv, kpe, rq, rk, rv, rg, ga, gb) = _in_proj(xp, tabs_p, seq // tq_p, w, tq_p)
        o_mla = _attn_prompt(qcat, kcat, w["w_uv"], batch, seq, tq_attn)
        o_ret, s_pair = _ret_prompt(rq, rk, rv, rg, ret_tabs_p, avg, batch, seq, chunk)
        xp, conv_new = _merge_ffn(xp, o_mla, o_ret, ga, gb, w, tq_p, seq)
        outs[0].append(ckv.reshape(batch, seq, KV_LORA))
        outs[1].append(kpe.reshape(batch, seq, ROPE))
        outs[2].append(_pair_state_to_heads(s_pair))
        outs[3].append(conv_new)

        (qcat, kcat, ckv, kpe, rq, rk, rv, rg, ga, gb) = _in_proj(xs, tabs_s, 1, w, tq_s)
        q4 = qcat.reshape(N_HEADS, bdec, t_new, KCAT).transpose(1, 0, 2, 3)
        q_lat = q4[..., :KV_LORA].reshape(bdec, N_HEADS * t_new, KV_LORA)
        q_pe = q4[..., KV_LORA:].astype(F32).reshape(bdec, N_HEADS, t_new, LANES // ROPE, ROPE).sum(axis=3)
        q_pe = q_pe.astype(BF16).reshape(bdec, N_HEADS * t_new, ROPE)
        pad = ((0, 0), (0, PAGE_SIZE - t_new), (0, 0))
        kn_c = jnp.pad(ckv.reshape(bdec, t_new, KV_LORA).astype(BF16), pad)
        kn_p = jnp.pad(kpe.reshape(bdec, t_new, ROPE).astype(BF16), pad)
        o_mla = _attn_sample(page_table, q_lat, q_pe, kn_c, kn_p, w["w_uv"], cache_ckv, cache_kpe_t, l,
                             chunk_pages).reshape(n_s, N_HEADS * V_MLA)
        o_ret, s_new = _ret_sample(rq, rk, rv, rg, state_ret, l, ret_tabs_s, seqmask, avg, ret_nseq, t_new)
        buf = state_conv[l]
        zrow = jnp.zeros((bdec, t_new - 1, FFN_DIM), F32)
        e1 = jnp.concatenate([buf[:, 1:2], zrow], axis=1).reshape(n_s, FFN_DIM)
        e2 = jnp.concatenate([buf, zrow[:, 1:]], axis=1).reshape(n_s, FFN_DIM)
        xs, a_full = _merge_ffn(xs, o_mla, o_ret, ga, gb, w, tq_s, t_new, bounds=(e1, e2))
        outs[4].append(ckv.reshape(bdec, t_new, KV_LORA))
        outs[5].append(kpe.reshape(bdec, t_new, ROPE))
        outs[6].append(s_new)
        outs[7].append(a_full.reshape(bdec, t_new, FFN_DIM)[:, t_new - (CONV_W - 1):])

    stacked = [jnp.stack(o) for o in outs]
    return (xp.reshape(batch, seq, D_MODEL), xs.reshape(bdec, t_new, D_MODEL), *stacked)
```

```python
import functools

import jax
import jax.numpy as jnp
from jax import lax
from jax.experimental import pallas as pl
from jax.experimental.pallas import tpu as pltpu

F32 = jnp.float32
BF16 = jnp.bfloat16

D_MODEL = 1024
N_HEADS = 8
Q_LORA = 384
KV_LORA = 256
NOPE = 64
ROPE = 32
V_MLA = 64
DK_RET = 64
DV_RET = 64
FFN_DIM = 2816
CONV_W = 3
PAGE_SIZE = 128
MLA_SCALE = (NOPE + ROPE) ** -0.5
LOG2_E = 1.4426950408889634
ROPE_THETA = 10000.0
EPS = 1e-6
GN_EPS = 1e-5
NEG_INF = -1e30

LANES = 128
SUBLANES = 8
VMEM_LIMIT_BYTES = 56 * 1024 * 1024

COL_CQ = 0
COL_CKV = COL_CQ + Q_LORA
COL_RQ = COL_CKV + KV_LORA
COL_RK = COL_RQ + N_HEADS * DK_RET
COL_RV = COL_RK + N_HEADS * DK_RET
COL_RG = COL_RV + N_HEADS * DV_RET
COL_GA = COL_RG + N_HEADS * DV_RET
COL_GB = COL_GA + D_MODEL
COL_KPE = COL_GB + D_MODEL
IN_COLS = COL_KPE + LANES
KCAT = KV_LORA + LANES
RET_W = N_HEADS * DK_RET
N_PAIRS = N_HEADS // 2


def _const_spec(shape, layer=None):
    nd = len(shape)
    if layer is None:
        return pl.BlockSpec(shape, lambda *_: (0,) * nd, pipeline_mode=pl.Buffered(1))
    return pl.BlockSpec((None,) + tuple(shape), lambda *_: (layer,) + (0,) * nd,
                        pipeline_mode=pl.Buffered(1))


def _rms(x, g):
    return x * lax.rsqrt(jnp.mean(x * x, axis=-1, keepdims=True) + EPS) * g


def _rope_tile(x, cos, sin_lo, sin_hi, half):
    up = pltpu.roll(x, LANES - half, 1)
    down = pltpu.roll(x, half, 1)
    return x * cos + up * sin_lo + down * sin_hi


def _dot(a, b):
    return jnp.dot(a, b, preferred_element_type=F32)


def _dot_nt(a, b):
    return lax.dot_general(a, b, (((1,), (1,)), ((), ())), preferred_element_type=F32)


def _dot_tn(a, b):
    return lax.dot_general(a, b, (((0,), (0,)), ((), ())), preferred_element_type=F32)


def _in_proj_kernel(x_ref, gpre_ref, win_ref, gq_ref, wuq_ref, gkv_ref, wuk_ref,
                    cr_ref, slr_ref, shr_ref, cm_ref, slm_ref, shm_ref,
                    qcat_ref, kcat_ref, ckv_ref, kpe_ref, rq_ref, rk_ref, rv_ref,
                    rg_ref, ga_ref, gb_ref, *rest):
    xn_scr = rest[-1]
    xn_scr[...] = _rms(x_ref[...], gpre_ref[...]).astype(BF16)

    def proj(lo, width):
        return _dot(xn_scr[...], win_ref[:, lo:lo + width])

    cq = _rms(proj(COL_CQ, Q_LORA), gq_ref[...]).astype(BF16)
    q = _dot(cq, wuq_ref[...]) * (MLA_SCALE * LOG2_E)
    for p in range(N_PAIRS):
        qp = q[:, p * LANES:(p + 1) * LANES].astype(BF16)
        for e in range(2):
            h = 2 * p + e
            qcat_ref[h, :, 0:KV_LORA] = _dot(qp, wuk_ref[h]).astype(BF16)
    lane = lax.broadcasted_iota(jnp.int32, (1, LANES), 1)
    heads_per_tile = LANES // ROPE
    pe_off = N_HEADS * NOPE
    for g in range(N_HEADS // heads_per_tile):
        sl = slice(g * LANES, (g + 1) * LANES)
        blk = _rope_tile(q[:, pe_off + g * LANES:pe_off + (g + 1) * LANES],
                         cm_ref[:, sl], slm_ref[:, sl], shm_ref[:, sl], ROPE // 2)
        for e in range(heads_per_tile):
            h = heads_per_tile * g + e
            keep = (lane >= e * ROPE) & (lane < (e + 1) * ROPE)
            qcat_ref[h, :, KV_LORA:KCAT] = jnp.where(keep, blk, 0.0).astype(BF16)

    ckv = _rms(proj(COL_CKV, KV_LORA), gkv_ref[...])
    ckv_ref[...] = ckv
    kcat_ref[:, 0:KV_LORA] = ckv.astype(BF16)
    if len(rest) == 2:
        rest[0][...] = ckv.T.astype(BF16)
    kp = _rope_tile(proj(COL_KPE, LANES), cm_ref[:, 0:LANES], slm_ref[:, 0:LANES],
                    shm_ref[:, 0:LANES], ROPE // 2)
    kcat_ref[:, KV_LORA:KCAT] = kp.astype(BF16)
    kpe_ref[...] = kp[:, 0:ROPE]

    for j in range(RET_W // LANES):
        sl = slice(j * LANES, (j + 1) * LANES)
        rq = _rope_tile(proj(COL_RQ + j * LANES, LANES), cr_ref[:, sl], slr_ref[:, sl],
                        shr_ref[:, sl], DK_RET // 2)
        rq_ref[:, sl] = rq.astype(rq_ref.dtype)
        rk = _rope_tile(proj(COL_RK + j * LANES, LANES), cr_ref[:, sl], slr_ref[:, sl],
                        shr_ref[:, sl], DK_RET // 2) * (DK_RET ** -0.5)
        rk_ref[:, sl] = rk.astype(rk_ref.dtype)
    rv_ref[...] = proj(COL_RV, RET_W).astype(rv_ref.dtype)
    rg = proj(COL_RG, RET_W)
    rg_ref[...] = (rg * jax.nn.sigmoid(rg)).astype(rg_ref.dtype)
    ga_ref[...] = jax.nn.sigmoid(proj(COL_GA, D_MODEL)).astype(ga_ref.dtype)
    gb_ref[...] = jax.nn.sigmoid(proj(COL_GB, D_MODEL)).astype(gb_ref.dtype)


def _in_proj(x2d, tabs, tab_blocks, w, layer, tq, with_kv_t):
    n = x2d.shape[0]
    grid = (n // tq,)
    row = lambda width: pl.BlockSpec((tq, width), lambda i: (i, 0))
    tab = lambda width: pl.BlockSpec((tq, width), lambda i: (i % tab_blocks, 0))
    in_specs = [
        row(D_MODEL), _const_spec((1, D_MODEL), layer), _const_spec((D_MODEL, IN_COLS), layer),
        _const_spec((1, Q_LORA), layer), _const_spec((Q_LORA, N_HEADS * (NOPE + ROPE)), layer),
        _const_spec((1, KV_LORA), layer), _const_spec((N_HEADS, LANES, KV_LORA), layer),
        tab(RET_W), tab(RET_W), tab(RET_W),
        tab(N_HEADS * ROPE), tab(N_HEADS * ROPE), tab(N_HEADS * ROPE),
    ]
    out_shape = [
        jax.ShapeDtypeStruct((N_HEADS, n, KCAT), BF16),
        jax.ShapeDtypeStruct((n, KCAT), BF16),
        jax.ShapeDtypeStruct((n, KV_LORA), F32),
        jax.ShapeDtypeStruct((n, ROPE), F32),
        jax.ShapeDtypeStruct((n, RET_W), BF16),
        jax.ShapeDtypeStruct((n, RET_W), BF16),
        jax.ShapeDtypeStruct((n, RET_W), BF16),
        jax.ShapeDtypeStruct((n, RET_W), BF16),
        jax.ShapeDtypeStruct((n, D_MODEL), BF16),
        jax.ShapeDtypeStruct((n, D_MODEL), BF16),
    ]
    out_specs = [
        pl.BlockSpec((N_HEADS, tq, KCAT), lambda i: (0, i, 0)),
        row(KCAT), row(KV_LORA), row(ROPE), row(RET_W), row(RET_W), row(RET_W), row(RET_W),
        row(D_MODEL), row(D_MODEL),
    ]
    if with_kv_t:
        out_shape.append(jax.ShapeDtypeStruct((KV_LORA, n), BF16))
        out_specs.append(pl.BlockSpec((KV_LORA, tq), lambda i: (0, i)))
    return pl.pallas_call(
        _in_proj_kernel, grid=grid, in_specs=in_specs, out_specs=out_specs, out_shape=out_shape,
        scratch_shapes=[pltpu.VMEM((tq, D_MODEL), BF16)],
        compiler_params=pltpu.CompilerParams(dimension_semantics=("parallel",),
                                             vmem_limit_bytes=VMEM_LIMIT_BYTES),
        name="in_proj",
    )(x2d, w["g_mix_pre"], w["w_in"], w["g_q"], w["w_uq"], w["g_kv"], w["w_uk"], *tabs)


def _attn_prompt_kernel(qi_ref, ki_ref, q_ref, k_ref, vt_ref, wuvt_ref, o_ref, m_scr, l_scr, acc_scr):
    step = pl.program_id(1)
    qi = qi_ref[step]
    ki = ki_ref[step]
    tq = q_ref.shape[1]

    @pl.when(ki == 0)
    def _():
        m_scr[...] = jnp.full(m_scr.shape, NEG_INF, F32)
        l_scr[...] = jnp.zeros(l_scr.shape, F32)
        acc_scr[...] = jnp.zeros(acc_scr.shape, F32)

    def update(masked):
        k = k_ref[...]
        vt = vt_ref[...]
        if masked:
            key = lax.broadcasted_iota(jnp.int32, (tq, tq), 0)
            qry = lax.broadcasted_iota(jnp.int32, (tq, tq), 1)
            causal = key <= qry
        for h in range(N_HEADS):
            st = _dot_nt(k, q_ref[h])
            if masked:
                st = jnp.where(causal, st, NEG_INF)
            m_old = m_scr[h]
            m_new = jnp.maximum(m_old, jnp.max(st, axis=0, keepdims=True))
            alpha = jnp.exp2(m_old - m_new)
            p = jnp.exp2(st - m_new)
            l_scr[h] = alpha * l_scr[h] + jnp.sum(p, axis=0, keepdims=True)
            acc_scr[h] = alpha * acc_scr[h] + _dot(vt, p.astype(BF16))
            m_scr[h] = m_new

    @pl.when(ki < qi)
    def _():
        update(False)

    @pl.when(ki == qi)
    def _():
        update(True)

        def o_t(h):
            return (acc_scr[h] * (1.0 / l_scr[h])).astype(BF16)

        tiles = [_dot(wuvt_ref[2 * p], o_t(2 * p)) + _dot(wuvt_ref[2 * p + 1], o_t(2 * p + 1))
                 for p in range(N_PAIRS)]
        o_ref[...] = jnp.concatenate(tiles, axis=0).T.astype(o_ref.dtype)


def _attn_prompt(qcat, kcat, ckv_t, wuv_t, layer, batch, seq, tq):
    nq = seq // tq
    pairs = [(qi, ki) for qi in range(nq) for ki in range(qi + 1)]
    qi_tab = jnp.asarray([p[0] for p in pairs], jnp.int32)
    ki_tab = jnp.asarray([p[1] for p in pairs], jnp.int32)
    n = batch * seq
    grid_spec = pltpu.PrefetchScalarGridSpec(
        num_scalar_prefetch=2,
        grid=(batch, len(pairs)),
        in_specs=[
            pl.BlockSpec((N_HEADS, tq, KCAT), lambda b, s, qt, kt: (0, b * nq + qt[s], 0)),
            pl.BlockSpec((tq, KCAT), lambda b, s, qt, kt: (b * nq + kt[s], 0)),
            pl.BlockSpec((KV_LORA, tq), lambda b, s, qt, kt: (0, b * nq + kt[s])),
            pl.BlockSpec((None, N_HEADS, LANES, KV_LORA), lambda b, s, qt, kt: (layer, 0, 0, 0),
                         pipeline_mode=pl.Buffered(1)),
        ],
        out_specs=pl.BlockSpec((tq, N_HEADS * V_MLA), lambda b, s, qt, kt: (b * nq + qt[s], 0)),
        scratch_shapes=[pltpu.VMEM((N_HEADS, 1, tq), F32), pltpu.VMEM((N_HEADS, 1, tq), F32),
                        pltpu.VMEM((N_HEADS, KV_LORA, tq), F32)],
    )
    return pl.pallas_call(
        _attn_prompt_kernel, grid_spec=grid_spec,
        out_shape=jax.ShapeDtypeStruct((n, N_HEADS * V_MLA), BF16),
        compiler_params=pltpu.CompilerParams(dimension_semantics=("parallel", "arbitrary"),
                                             vmem_limit_bytes=VMEM_LIMIT_BYTES),
        name="attn_prompt",
    )(qi_tab, ki_tab, qcat, kcat, ckv_t, wuv_t)


def _attn_sample_kernel(pt_ref, ql_ref, qp_ref, kn_c_ref, kn_p_ref, wuv_ref, ckv_hbm, kpe_hbm, o_ref,
                        ckv_buf, kpe_buf, sem, *, layer, n_pages, chunk_pages, block_pages):
    b = pl.program_id(0)
    n_chunks = n_pages // chunk_pages
    ql = ql_ref[0]
    qp = qp_ref[0]
    t_new = ql.shape[0] // N_HEADS

    def chunk_copies(seq, j, slot):
        copies = []
        for i in range(chunk_pages):
            page = pt_ref[seq * n_pages + j * chunk_pages + i]
            copies.append(pltpu.make_async_copy(ckv_hbm.at[layer, page], ckv_buf.at[slot, i], sem.at[slot]))
            copies.append(pltpu.make_async_copy(kpe_hbm.at[layer, page], kpe_buf.at[slot, i], sem.at[slot]))
        return copies

    def start(copies):
        for c in copies:
            c.start()

    @pl.when(b == 0)
    def _():
        start(chunk_copies(b, 0, 0))

    def partial_softmax(s, v):
        m = jnp.max(s, axis=-1, keepdims=True)
        p = jnp.exp2(s - m)
        return m, jnp.sum(p, axis=-1, keepdims=True), _dot(p.astype(BF16), v)

    def merge(state, parts):
        m_new = functools.reduce(jnp.maximum, [m for m, _, _ in state + parts])
        l_new, acc_new = 0.0, 0.0
        for m, l, acc in state + parts:
            w = jnp.exp2(m - m_new)
            l_new = l_new + w * l
            acc_new = acc_new + w * acc
        return [(m_new, l_new, acc_new)]

    kc = kn_c_ref[0]
    s = _dot_nt(ql, kc) + _dot_nt(qp, kn_p_ref[0])
    r = lax.broadcasted_iota(jnp.int32, s.shape, 0) % t_new
    c = lax.broadcasted_iota(jnp.int32, s.shape, 1)
    state = [partial_softmax(jnp.where(c <= r, s, NEG_INF), kc)]

    for j in range(n_chunks):
        slot = j % 2
        if j + 1 < n_chunks:
            start(chunk_copies(b, j + 1, 1 - slot))
        else:
            @pl.when(b + 1 < pl.num_programs(0))
            def _():
                start(chunk_copies(b + 1, 0, 1 - slot))
        for cp in chunk_copies(b, j, slot):
            cp.wait()
        parts = []
        for u in range(0, chunk_pages, block_pages):
            kc = ckv_buf[slot, u:u + block_pages].reshape(block_pages * PAGE_SIZE, KV_LORA).astype(BF16)
            kp = jnp.concatenate([kpe_buf[slot, u + i] for i in range(block_pages)], axis=1).astype(BF16)
            parts.append(partial_softmax(_dot_nt(ql, kc) + _dot(qp, kp), kc))
        state = merge(state, parts)

    (_, l_fin, acc_fin), = state
    o = (acc_fin * (1.0 / l_fin)).astype(BF16)
    for p in range(N_PAIRS):
        even = _dot(o, wuv_ref[2 * p])[(2 * p) * t_new:(2 * p + 1) * t_new]
        odd = _dot(o, wuv_ref[2 * p + 1])[(2 * p + 1) * t_new:(2 * p + 2) * t_new]
        o_ref[0, :, p * LANES:(p + 1) * LANES] = even + odd


def _attn_sample(page_table, q_lat, q_pe, kn_c, kn_p, wuv, cache_ckv, cache_kpe_t, layer, chunk_pages,
                 block_pages):
    bdec, rows, _ = q_lat.shape
    n_pages = page_table.shape[1]
    assert n_pages % (2 * chunk_pages) == 0 and chunk_pages % block_pages == 0
    t_new = rows // N_HEADS
    pt_flat = page_table.reshape(-1)
    per_seq = lambda r, width: pl.BlockSpec((1, r, width), lambda b, pt: (b, 0, 0))
    grid_spec = pltpu.PrefetchScalarGridSpec(
        num_scalar_prefetch=1,
        grid=(bdec,),
        in_specs=[per_seq(rows, KV_LORA), per_seq(rows, ROPE),
                  per_seq(PAGE_SIZE, KV_LORA), per_seq(PAGE_SIZE, ROPE),
                  pl.BlockSpec((None, N_HEADS, KV_LORA, LANES), lambda b, pt: (layer, 0, 0, 0),
                               pipeline_mode=pl.Buffered(1)),
                  pl.BlockSpec(memory_space=pl.ANY), pl.BlockSpec(memory_space=pl.ANY)],
        out_specs=pl.BlockSpec((1, t_new, N_HEADS * V_MLA), lambda b, pt: (b, 0, 0)),
        scratch_shapes=[pltpu.VMEM((2, chunk_pages, PAGE_SIZE, KV_LORA), F32),
                        pltpu.VMEM((2, chunk_pages, ROPE, PAGE_SIZE), F32),
                        pltpu.SemaphoreType.DMA((2,))],
    )
    return pl.pallas_call(
        functools.partial(_attn_sample_kernel, layer=layer, n_pages=n_pages, chunk_pages=chunk_pages,
                          block_pages=block_pages),
        grid_spec=grid_spec,
        out_shape=jax.ShapeDtypeStruct((bdec, t_new, N_HEADS * V_MLA), F32),
        compiler_params=pltpu.CompilerParams(dimension_semantics=("arbitrary",),
                                             vmem_limit_bytes=VMEM_LIMIT_BYTES),
        name="attn_sample",
    )(pt_flat, q_lat, q_pe, kn_c, kn_p, wuv, cache_ckv, cache_kpe_t)


def _ret_intra_pair(q, k, v, d_even, d_odd, left):
    zero = jnp.zeros_like(q)
    s0 = _dot_nt(jnp.where(left, q, zero), k)
    s1 = _dot_nt(jnp.where(left, zero, q), k)
    p0 = (s0 * d_even).astype(BF16)
    p1 = (s1 * d_odd).astype(BF16)
    return _dot(p0, jnp.where(left, v, zero)) + _dot(p1, jnp.where(left, zero, v))


def _group_norm_gate(o, avg_ref, gate):
    mu = _dot(o.astype(BF16), avg_ref[...])
    d = o - mu
    var = _dot((d * d).astype(BF16), avg_ref[...])
    return d * lax.rsqrt(var + GN_EPS) * gate


def _ret_prompt_kernel(q_ref, k_ref, v_ref, gate_ref, dec_ref, qd_ref, kd_ref, gc_ref, avg_ref,
                       o_ref, s_out_ref, s_scr):
    c = pl.program_id(1)

    @pl.when(c == 0)
    def _():
        s_scr[...] = jnp.zeros(s_scr.shape, F32)

    lane = lax.broadcasted_iota(jnp.int32, (1, LANES), 1)
    left = lane < DK_RET
    rr = lax.broadcasted_iota(jnp.int32, (LANES, LANES), 0)
    cc = lax.broadcasted_iota(jnp.int32, (LANES, LANES), 1)
    same_head = (rr < DK_RET) == (cc < DV_RET)
    outs = []
    for p in range(N_PAIRS):
        sl = slice(p * LANES, (p + 1) * LANES)
        q = q_ref[:, sl]
        k = k_ref[:, sl]
        v = v_ref[:, sl]
        inner = _ret_intra_pair(q, k, v, dec_ref[2 * p], dec_ref[2 * p + 1], left)
        state = s_scr[p]
        cross = _dot(q, state.astype(BF16)) * qd_ref[p]
        outs.append(inner + cross)
        kd = (k.astype(F32) * kd_ref[p]).astype(BF16)
        upd = _dot_tn(kd, v)
        s_scr[p] = gc_ref[p] * state + jnp.where(same_head, upd, 0.0)
    o = jnp.concatenate(outs, axis=-1)
    o_ref[...] = _group_norm_gate(o, avg_ref, gate_ref[...].astype(F32)).astype(o_ref.dtype)

    @pl.when(c == pl.num_programs(1) - 1)
    def _():
        s_out_ref[0] = s_scr[...]


def _ret_prompt(rq, rk, rv, gate, tabs, avg, batch, seq, chunk):
    nc = seq // chunk
    n = batch * seq
    row = pl.BlockSpec((chunk, RET_W), lambda b, c: (b * nc + c, 0))
    dec, qd, kd, gc = tabs
    return pl.pallas_call(
        _ret_prompt_kernel, grid=(batch, nc),
        in_specs=[row, row, row, row, _const_spec(dec.shape), _const_spec(qd.shape),
                  _const_spec(kd.shape), _const_spec(gc.shape), _const_spec(avg.shape)],
        out_specs=(row, pl.BlockSpec((1, N_PAIRS, LANES, LANES), lambda b, c: (b, 0, 0, 0))),
        out_shape=(jax.ShapeDtypeStruct((n, RET_W), BF16),
                   jax.ShapeDtypeStruct((batch, N_PAIRS, LANES, LANES), F32)),
        scratch_shapes=[pltpu.VMEM((N_PAIRS, LANES, LANES), F32)],
        compiler_params=pltpu.CompilerParams(dimension_semantics=("parallel", "arbitrary"),
                                             vmem_limit_bytes=VMEM_LIMIT_BYTES),
        name="ret_prompt",
    )(rq, rk, rv, gate, dec, qd, kd, gc, avg)


def _ret_sample_kernel(q_ref, k_ref, v_ref, gate_ref, s_ref, dec_ref, qd_ref, kd_ref, gc_ref,
                       seqmask_ref, avg_ref, o_ref, s_out_ref):
    nseq = s_ref.shape[0]
    lane = lax.broadcasted_iota(jnp.int32, (1, LANES), 1)
    left = lane < DK_RET
    seqmask = seqmask_ref[...]
    reps = nseq * DK_RET // LANES
    outs = []
    for p in range(N_PAIRS):
        sl = slice(p * LANES, (p + 1) * LANES)
        q = q_ref[:, sl]
        k = k_ref[:, sl]
        v = v_ref[:, sl]
        inner = _ret_intra_pair(q, k, v, dec_ref[2 * p], dec_ref[2 * p + 1], left)
        qf = q.astype(F32)
        kf = k.astype(F32) * kd_ref[p]
        q_sw = pltpu.roll(qf, DK_RET, 1)
        k_sw = pltpu.roll(kf, DK_RET, 1)
        cross = []
        for e in range(2):
            h = 2 * p + e
            keep = left if e == 0 else jnp.logical_not(left)
            qh = jnp.where(keep, qf, q_sw)
            kh = jnp.where(keep, kf, k_sw)
            q_exp = (jnp.concatenate([qh] * reps, axis=-1) * seqmask).astype(BF16)
            k_exp = (jnp.concatenate([kh] * reps, axis=-1) * seqmask).astype(BF16)
            state = s_ref[:, h].reshape(nseq * DK_RET, DV_RET)
            cross.append(_dot(q_exp, state.astype(BF16)) * qd_ref[h])
            vh = v[:, e * DV_RET:(e + 1) * DV_RET]
            upd = _dot_tn(k_exp, vh)
            s_out_ref[:, h] = (gc_ref[h] * state + upd).reshape(nseq, DK_RET, DV_RET)
        outs.append(inner + jnp.concatenate(cross, axis=-1))
    o = jnp.concatenate(outs, axis=-1)
    o_ref[...] = _group_norm_gate(o, avg_ref, gate_ref[...].astype(F32)).astype(o_ref.dtype)


def _ret_sample(rq, rk, rv, gate, state, layer, tabs, seqmask, avg, nseq, t_new):
    n = rq.shape[0]
    rows = nseq * t_new
    bdec = n // t_new
    row = pl.BlockSpec((rows, RET_W), lambda i: (i, 0))
    st_in = pl.BlockSpec((None, nseq, N_HEADS, DK_RET, DV_RET), lambda i: (layer, i, 0, 0, 0))
    st_out = pl.BlockSpec((nseq, N_HEADS, DK_RET, DV_RET), lambda i: (i, 0, 0, 0))
    dec, qd, kd, gc = tabs
    return pl.pallas_call(
        _ret_sample_kernel, grid=(bdec // nseq,),
        in_specs=[row, row, row, row, st_in, _const_spec(dec.shape), _const_spec(qd.shape),
                  _const_spec(kd.shape), _const_spec(gc.shape), _const_spec(seqmask.shape),
                  _const_spec(avg.shape)],
        out_specs=(row, st_out),
        out_shape=(jax.ShapeDtypeStruct((n, RET_W), BF16),
                   jax.ShapeDtypeStruct((bdec, N_HEADS, DK_RET, DV_RET), F32)),
        compiler_params=pltpu.CompilerParams(dimension_semantics=("parallel",),
                                             vmem_limit_bytes=VMEM_LIMIT_BYTES),
        name="ret_sample",
    )(rq, rk, rv, gate, state, dec, qd, kd, gc, seqmask, avg)


FFN_CHUNK = 256


def _merge_ffn_kernel(*refs, long_seq, tiles_per_seq, seq_rows):
    if long_seq:
        (x_ref, om_ref, orr_ref, ga_ref, gb_ref, wom_ref, wor_ref, wout_ref, gpost_ref, gfpre_ref,
         wup_ref, cw_ref, cb_ref, wdn_ref, gfpost_ref, y_ref, cnew_ref, a_scr, h_scr, xn_scr) = refs
    else:
        (x_ref, om_ref, orr_ref, ga_ref, gb_ref, wom_ref, wor_ref, wout_ref, gpost_ref, gfpre_ref,
         wup_ref, cw_ref, cb_ref, wdn_ref, gfpost_ref, e1_ref, e2_ref, y_ref, a_ref,
         a_scr, h_scr, xn_scr) = refs
    tq = x_ref.shape[0]
    a1 = _dot(om_ref[...].astype(BF16), wom_ref[...])
    a2 = _dot(orr_ref[...].astype(BF16), wor_ref[...])
    mixed = (ga_ref[...].astype(F32) * a1 + gb_ref[...].astype(F32) * a2).astype(BF16)
    x1 = x_ref[...] + _rms(_dot(mixed, wout_ref[...]), gpost_ref[...])
    xn_scr[...] = _rms(x1, gfpre_ref[...]).astype(BF16)

    @pl.when(pl.program_id(0) % tiles_per_seq == 0)
    def _():
        a_scr[0:SUBLANES, :] = jnp.zeros((SUBLANES, FFN_DIM), F32)

    if not long_seq:
        row_in_seq = lax.broadcasted_iota(jnp.int32, (tq, 1), 0) % seq_rows

    for c0 in range(0, FFN_DIM, FFN_CHUNK):
        sl = slice(c0, c0 + FFN_CHUNK)
        a = _dot(xn_scr[...], wup_ref[:, c0:c0 + FFN_CHUNK])
        b = _dot(xn_scr[...], wup_ref[:, FFN_DIM + c0:FFN_DIM + c0 + FFN_CHUNK])
        a_scr[SUBLANES:SUBLANES + tq, sl] = a
        a_m1 = a_scr[SUBLANES - 1:SUBLANES - 1 + tq, sl]
        a_m2 = a_scr[SUBLANES - 2:SUBLANES - 2 + tq, sl]
        if not long_seq:
            a_m1 = jnp.where(row_in_seq == 0, e1_ref[:, sl], a_m1)
            a_m2 = jnp.where(row_in_seq < 2, e2_ref[:, sl], a_m2)
            a_ref[:, sl] = a
        conv = cb_ref[:, sl] + a_m2 * cw_ref[0:1, sl] + a_m1 * cw_ref[1:2, sl] + a * cw_ref[2:3, sl]
        h_scr[:, sl] = (jax.nn.gelu(conv, approximate=True) * b).astype(BF16)
    f = _dot(h_scr[...], wdn_ref[...])
    y_ref[...] = x1 + _rms(f, gfpost_ref[...])
    if long_seq:
        cnew_ref[0] = a_scr[tq + SUBLANES - (CONV_W - 1):tq + SUBLANES, :]
        a_scr[0:SUBLANES, :] = a_scr[tq:tq + SUBLANES, :]


def _merge_ffn(x2d, o_mla, o_ret, ga, gb, w, layer, tq, seq, bounds=None):
    n = x2d.shape[0]
    long_seq = bounds is None
    assert (seq % tq == 0) if long_seq else (tq % seq == 0)
    row = lambda width: pl.BlockSpec((tq, width), lambda i: (i, 0))
    cs = lambda shape: _const_spec(shape, layer)
    in_specs = [row(D_MODEL), row(N_HEADS * V_MLA), row(RET_W), row(D_MODEL), row(D_MODEL),
                cs((N_HEADS * V_MLA, D_MODEL)), cs((RET_W, D_MODEL)),
                cs((D_MODEL, D_MODEL)), cs((1, D_MODEL)), cs((1, D_MODEL)),
                cs((D_MODEL, 2 * FFN_DIM)), cs((CONV_W, FFN_DIM)),
                cs((1, FFN_DIM)), cs((FFN_DIM, D_MODEL)), cs((1, D_MODEL))]
    args = [x2d, o_mla, o_ret, ga, gb, w["w_o_mla"], w["w_o_ret"], w["w_out"], w["g_mix_post"],
            w["g_ffn_pre"], w["w_up"], w["conv_w"], w["conv_b"], w["w_down"], w["g_ffn_post"]]
    if long_seq:
        tiles_per_seq = seq // tq
        out_shape = (jax.ShapeDtypeStruct((n, D_MODEL), F32),
                     jax.ShapeDtypeStruct((n // seq, CONV_W - 1, FFN_DIM), F32))
        out_specs = (row(D_MODEL),
                     pl.BlockSpec((1, CONV_W - 1, FFN_DIM), lambda i: (i // tiles_per_seq, 0, 0)))
    else:
        tiles_per_seq = n // tq
        in_specs += [row(FFN_DIM), row(FFN_DIM)]
        args += list(bounds)
        out_shape = (jax.ShapeDtypeStruct((n, D_MODEL), F32), jax.ShapeDtypeStruct((n, FFN_DIM), F32))
        out_specs = (row(D_MODEL), row(FFN_DIM))
    return pl.pallas_call(
        functools.partial(_merge_ffn_kernel, long_seq=long_seq, tiles_per_seq=tiles_per_seq,
                          seq_rows=seq),
        grid=(n // tq,), in_specs=in_specs, out_specs=out_specs, out_shape=out_shape,
        scratch_shapes=[pltpu.VMEM((tq + SUBLANES, FFN_DIM), F32), pltpu.VMEM((tq, FFN_DIM), BF16),
                        pltpu.VMEM((tq, D_MODEL), BF16)],
        compiler_params=pltpu.CompilerParams(dimension_semantics=("arbitrary",),
                                             vmem_limit_bytes=VMEM_LIMIT_BYTES),
        name="merge_ffn",
    )(*args)


def _rope_tables(pos, d, reps):
    half = d // 2
    inv = jnp.power(ROPE_THETA, -jnp.arange(half, dtype=F32) * (2.0 / d))
    ang = pos.astype(F32)[:, None] * inv[None, :]
    cos, sin = jnp.cos(ang), jnp.sin(ang)
    zero = jnp.zeros_like(sin)
    rep = lambda a: jnp.tile(a, (1, reps))
    return (rep(jnp.concatenate([cos, cos], -1)), rep(jnp.concatenate([-sin, zero], -1)),
            rep(jnp.concatenate([zero, sin], -1)))


def _log_decay():
    return jnp.log1p(-jnp.exp2(-5.0 - jnp.arange(N_HEADS, dtype=F32)))


def _pair_lanes(per_head):
    c = per_head.shape[1]
    x = jnp.broadcast_to(per_head[:, :, None], (N_HEADS, c, DK_RET))
    return x.reshape(N_PAIRS, 2, c, DK_RET).transpose(0, 2, 1, 3).reshape(N_PAIRS, c, LANES)


def _ret_tables_prompt(chunk):
    log_g = _log_decay()
    idx = jnp.arange(chunk, dtype=F32)
    diff = idx[:, None] - idx[None, :]
    dec = jnp.where(diff[None] >= 0, jnp.exp(jnp.maximum(diff, 0.0)[None] * log_g[:, None, None]), 0.0)
    qd = _pair_lanes(jnp.exp((idx + 1.0)[None, :] * log_g[:, None]))
    kd = _pair_lanes(jnp.exp((chunk - 1.0 - idx)[None, :] * log_g[:, None]))
    gc = _pair_lanes(jnp.exp(chunk * log_g)[:, None])
    return dec, qd, kd, gc


def _ret_tables_sample(nseq, t_new):
    log_g = _log_decay()
    rows = nseq * t_new
    r = jnp.arange(rows)
    t = (r % t_new).astype(F32)
    same = (r[:, None] // t_new) == (r[None, :] // t_new)
    diff = t[:, None] - t[None, :]
    dec = jnp.where((same & (diff >= 0))[None],
                    jnp.exp(jnp.maximum(diff, 0.0)[None] * log_g[:, None, None]), 0.0)
    qd = jnp.exp((t + 1.0)[None, :] * log_g[:, None])[:, :, None]
    kd = _pair_lanes(jnp.exp((t_new - 1.0 - t)[None, :] * log_g[:, None]))
    gc = jnp.exp(t_new * log_g)[:, None, None]
    seqmask = ((r[:, None] // t_new) == (jnp.arange(nseq * DK_RET)[None, :] // DK_RET)).astype(F32)
    return (dec, qd, kd, gc), seqmask


def _group_avg():
    g = jnp.arange(RET_W) // DV_RET
    return jnp.where(g[:, None] == g[None, :], 1.0 / DV_RET, 0.0).astype(BF16)


def _prep_weights(norm_mix_pre, norm_mix_post, norm_ffn_pre, norm_ffn_post, w_in, norm_q_a, w_uq,
                  norm_kv_a, w_uk, w_uv, w_o_mla, w_o_ret, w_out, w_up, conv_w, conv_b, w_down):
    depth = w_in.shape[0]
    k0 = Q_LORA + KV_LORA
    w_in_r = jnp.concatenate([w_in[:, :, :k0], w_in[:, :, k0 + ROPE:],
                              jnp.tile(w_in[:, :, k0:k0 + ROPE], (1, 1, LANES // ROPE))], axis=-1)
    uq = w_uq.reshape(depth, Q_LORA, N_HEADS, NOPE + ROPE)
    w_uq_r = jnp.concatenate([uq[..., :NOPE].reshape(depth, Q_LORA, N_HEADS * NOPE),
                              uq[..., NOPE:].reshape(depth, Q_LORA, N_HEADS * ROPE)], axis=-1)
    uk = jnp.transpose(w_uk, (0, 2, 3, 1))
    odd = (jnp.arange(N_HEADS) % 2 == 1)[None, :, None, None]
    zeros = jnp.zeros_like(uk)
    w_uk_r = jnp.concatenate([jnp.where(odd, zeros, uk), jnp.where(odd, uk, zeros)], axis=2)
    uv = jnp.transpose(w_uv, (0, 2, 1, 3))
    zeros = jnp.zeros_like(uv)
    w_uv_r = jnp.concatenate([jnp.where(odd, zeros, uv), jnp.where(odd, uv, zeros)], axis=3)
    vec = lambda g: g[:, None, :]
    return dict(
        g_mix_pre=vec(norm_mix_pre), g_mix_post=vec(norm_mix_post), g_ffn_pre=vec(norm_ffn_pre),
        g_ffn_post=vec(norm_ffn_post), g_q=vec(norm_q_a), g_kv=vec(norm_kv_a),
        w_in=w_in_r.astype(BF16), w_uq=w_uq_r.astype(BF16), w_uk=w_uk_r.astype(BF16),
        w_uv=w_uv_r.astype(BF16), w_uv_t=jnp.swapaxes(w_uv_r, 2, 3).astype(BF16),
        w_o_mla=w_o_mla.astype(BF16), w_o_ret=w_o_ret.astype(BF16),
        w_out=w_out.astype(BF16), w_up=w_up.astype(BF16), conv_w=conv_w, conv_b=vec(conv_b),
        w_down=w_down.astype(BF16))


def _pair_state_to_heads(sp):
    even = sp[:, :, :DK_RET, :DV_RET]
    odd = sp[:, :, DK_RET:, DV_RET:]
    return jnp.stack([even, odd], axis=2).reshape(sp.shape[0], N_HEADS, DK_RET, DV_RET)


def _pick_tile(n, target):
    t = min(n, target)
    while n % t:
        t //= 2
    return t


def kernel(x_prompt, x_sample, cache_ckv, cache_kpe, state_ret, state_conv, page_table, norm_mix_pre, norm_mix_post, norm_ffn_pre, norm_ffn_post, w_in, norm_q_a, w_uq, norm_kv_a, w_uk, w_uv, w_o_mla, w_o_ret, w_out, w_up, conv_w, conv_b, w_down):
    batch, seq, _ = x_prompt.shape
    bdec, t_new, _ = x_sample.shape
    depth = w_in.shape[0]
    n_pages = page_table.shape[1]
    past_len = n_pages * PAGE_SIZE
    n_p, n_s = batch * seq, bdec * t_new

    tq_p = _pick_tile(seq, 256)
    tq_s = _pick_tile(n_s, 256)
    chunk = _pick_tile(seq, 256)
    ret_nseq = _pick_tile(bdec, LANES // t_new)
    tq_attn = _pick_tile(seq, 512)
    chunk_pages = _pick_tile(n_pages // 2, 32)
    block_pages = chunk_pages
    cache_kpe_t = jnp.swapaxes(cache_kpe, 2, 3)

    weights = _prep_weights(norm_mix_pre, norm_mix_post, norm_ffn_pre, norm_ffn_post, w_in, norm_q_a,
                            w_uq, norm_kv_a, w_uk, w_uv, w_o_mla, w_o_ret, w_out, w_up, conv_w, conv_b,
                            w_down)
    pos_p = jnp.arange(seq, dtype=jnp.int32)
    pos_s = past_len + (jnp.arange(tq_s, dtype=jnp.int32) % t_new)
    tabs_p = _rope_tables(pos_p, DK_RET, N_HEADS) + _rope_tables(pos_p, ROPE, N_HEADS)
    tabs_s = _rope_tables(pos_s, DK_RET, N_HEADS) + _rope_tables(pos_s, ROPE, N_HEADS)
    ret_tabs_p = _ret_tables_prompt(chunk)
    ret_tabs_s, seqmask = _ret_tables_sample(ret_nseq, t_new)
    avg = _group_avg()

    xp = x_prompt.reshape(n_p, D_MODEL)
    xs = x_sample.reshape(n_s, D_MODEL)
    outs = [[] for _ in range(8)]
    w = weights
    for l in range(depth):
        (qcat, kcat, ckv, kpe, rq, rk, rv, rg, ga, gb, ckv_t) = _in_proj(
            xp, tabs_p, seq // tq_p, w, l, tq_p, with_kv_t=True)
        o_mla = _attn_prompt(qcat, kcat, ckv_t, w["w_uv_t"], l, batch, seq, tq_attn)
        o_ret, s_pair = _ret_prompt(rq, rk, rv, rg, ret_tabs_p, avg, batch, seq, chunk)
        xp, conv_new = _merge_ffn(xp, o_mla, o_ret, ga, gb, w, l, tq_p, seq)
        outs[0].append(ckv.reshape(batch, seq, KV_LORA))
        outs[1].append(kpe.reshape(batch, seq, ROPE))
        outs[2].append(_pair_state_to_heads(s_pair))
        outs[3].append(conv_new)

        (qcat, kcat, ckv, kpe, rq, rk, rv, rg, ga, gb) = _in_proj(xs, tabs_s, 1, w, l, tq_s, with_kv_t=False)
        q4 = qcat.reshape(N_HEADS, bdec, t_new, KCAT).transpose(1, 0, 2, 3)
        q_lat = q4[..., :KV_LORA].reshape(bdec, N_HEADS * t_new, KV_LORA)
        q_pe = q4[..., KV_LORA:].astype(F32).reshape(bdec, N_HEADS, t_new, LANES // ROPE, ROPE).sum(axis=3)
        q_pe = q_pe.astype(BF16).reshape(bdec, N_HEADS * t_new, ROPE)
        pad = ((0, 0), (0, PAGE_SIZE - t_new), (0, 0))
        kn_c = jnp.pad(ckv.reshape(bdec, t_new, KV_LORA).astype(BF16), pad)
        kn_p = jnp.pad(kpe.reshape(bdec, t_new, ROPE).astype(BF16), pad)
        o_mla = _attn_sample(page_table, q_lat, q_pe, kn_c, kn_p, w["w_uv"], cache_ckv, cache_kpe_t, l,
                             chunk_pages, block_pages).reshape(n_s, N_HEADS * V_MLA)
        o_ret, s_new = _ret_sample(rq, rk, rv, rg, state_ret, l, ret_tabs_s, seqmask, avg, ret_nseq, t_new)
        buf = state_conv[l]
        zrow = jnp.zeros((bdec, t_new - 1, FFN_DIM), F32)
        e1 = jnp.concatenate([buf[:, 1:2], zrow], axis=1).reshape(n_s, FFN_DIM)
        e2 = jnp.concatenate([buf, zrow[:, 1:]], axis=1).reshape(n_s, FFN_DIM)
        xs, a_full = _merge_ffn(xs, o_mla, o_ret, ga, gb, w, l, tq_s, t_new, bounds=(e1, e2))
        outs[4].append(ckv.reshape(bdec, t_new, KV_LORA))
        outs[5].append(kpe.reshape(bdec, t_new, ROPE))
        outs[6].append(s_new)
        outs[7].append(a_full.reshape(bdec, t_new, FFN_DIM)[:, t_new - (CONV_W - 1):])

    stacked = [jnp.stack(o) for o in outs]
    return (xp.reshape(batch, seq, D_MODEL), xs.reshape(bdec, t_new, D_MODEL), *stacked)
```

```python
import functools

import jax
import jax.numpy as jnp
from jax import lax
from jax.experimental import pallas as pl
from jax.experimental.pallas import tpu as pltpu

F32 = jnp.float32
BF16 = jnp.bfloat16

D_MODEL = 1024
N_HEADS = 8
Q_LORA = 384
KV_LORA = 256
NOPE = 64
ROPE = 32
V_MLA = 64
DK_RET = 64
DV_RET = 64
FFN_DIM = 2816
CONV_W = 3
PAGE_SIZE = 128
MLA_SCALE = (NOPE + ROPE) ** -0.5
LOG2_E = 1.4426950408889634
ROPE_THETA = 10000.0
EPS = 1e-6
GN_EPS = 1e-5
NEG_INF = -1e30

LANES = 128
SUBLANES = 8
VMEM_LIMIT_BYTES = 56 * 1024 * 1024

COL_CQ = 0
COL_CKV = COL_CQ + Q_LORA
COL_RQ = COL_CKV + KV_LORA
COL_RK = COL_RQ + N_HEADS * DK_RET
COL_RV = COL_RK + N_HEADS * DK_RET
COL_RG = COL_RV + N_HEADS * DV_RET
COL_GA = COL_RG + N_HEADS * DV_RET
COL_GB = COL_GA + D_MODEL
COL_KPE = COL_GB + D_MODEL
IN_COLS = COL_KPE + LANES
KCAT = KV_LORA + LANES
RET_W = N_HEADS * DK_RET
N_PAIRS = N_HEADS // 2


def _const_spec(shape, layer=None):
    nd = len(shape)
    if layer is None:
        return pl.BlockSpec(shape, lambda *_: (0,) * nd, pipeline_mode=pl.Buffered(1))
    return pl.BlockSpec((None,) + tuple(shape), lambda *_: (layer,) + (0,) * nd,
                        pipeline_mode=pl.Buffered(1))


def _rms(x, g):
    return x * lax.rsqrt(jnp.mean(x * x, axis=-1, keepdims=True) + EPS) * g


def _rope_tile(x, cos, sin_lo, sin_hi, half):
    up = pltpu.roll(x, LANES - half, 1)
    down = pltpu.roll(x, half, 1)
    return x * cos + up * sin_lo + down * sin_hi


def _dot(a, b):
    return jnp.dot(a, b, preferred_element_type=F32)


def _dot_nt(a, b):
    return lax.dot_general(a, b, (((1,), (1,)), ((), ())), preferred_element_type=F32)


def _dot_tn(a, b):
    return lax.dot_general(a, b, (((0,), (0,)), ((), ())), preferred_element_type=F32)


def _in_proj_kernel(*refs, per_head):
    if per_head:
        (x_ref, gpre_ref, win_ref, gq_ref, wuq_ref, gkv_ref, wkh_ref, wuvt_ref,
         cr_ref, slr_ref, shr_ref, cm_ref, slm_ref, shm_ref, cq_ref, slq_ref, shq_ref,
         q_ref, k_ref, vt_ref, ckv_ref, kpe_ref, rq_ref, rk_ref, rv_ref, rg_ref, ga_ref, gb_ref,
         xn_scr) = refs
    else:
        (x_ref, gpre_ref, win_ref, gq_ref, wuq_ref, gkv_ref, wuk_ref,
         cr_ref, slr_ref, shr_ref, cm_ref, slm_ref, shm_ref,
         q_ref, ckv_ref, kpe_ref, rq_ref, rk_ref, rv_ref, rg_ref, ga_ref, gb_ref, xn_scr) = refs
    xn_scr[...] = _rms(x_ref[...], gpre_ref[...]).astype(BF16)

    def proj(lo, width):
        return _dot(xn_scr[...], win_ref[:, lo:lo + width])

    ckv = _rms(proj(COL_CKV, KV_LORA), gkv_ref[...])
    ckv_ref[...] = ckv
    kp = _rope_tile(proj(COL_KPE, LANES), cm_ref[:, 0:LANES], slm_ref[:, 0:LANES],
                    shm_ref[:, 0:LANES], ROPE // 2)
    kpe_ref[...] = kp[:, 0:ROPE]

    cq = _rms(proj(COL_CQ, Q_LORA), gq_ref[...]).astype(BF16)
    q = _dot(cq, wuq_ref[...]) * (MLA_SCALE * LOG2_E)
    lane = lax.broadcasted_iota(jnp.int32, (1, LANES), 1)
    if per_head:
        ckv_b = ckv.astype(BF16)
        kp_own = jnp.where((lane >= NOPE) & (lane < NOPE + ROPE), kp, 0.0)
        for h in range(N_HEADS):
            q_ref[h] = _rope_tile(q[:, h * LANES:(h + 1) * LANES], cq_ref[...], slq_ref[...],
                                  shq_ref[...], ROPE // 2).astype(BF16)
            k_ref[h] = (_dot(ckv_b, wkh_ref[h]) + kp_own).astype(BF16)
        vt_ref[...] = _dot(wuvt_ref[...], ckv.T.astype(BF16)).astype(BF16)
    else:
        for p in range(N_PAIRS):
            qp = q[:, p * LANES:(p + 1) * LANES].astype(BF16)
            for e in range(2):
                h = 2 * p + e
                q_ref[h, :, 0:KV_LORA] = _dot(qp, wuk_ref[h]).astype(BF16)
        heads_per_tile = LANES // ROPE
        pe_off = N_HEADS * NOPE
        for g in range(N_HEADS // heads_per_tile):
            sl = slice(g * LANES, (g + 1) * LANES)
            blk = _rope_tile(q[:, pe_off + g * LANES:pe_off + (g + 1) * LANES],
                             cm_ref[:, sl], slm_ref[:, sl], shm_ref[:, sl], ROPE // 2)
            for e in range(heads_per_tile):
                h = heads_per_tile * g + e
                keep = (lane >= e * ROPE) & (lane < (e + 1) * ROPE)
                q_ref[h, :, KV_LORA:KCAT] = jnp.where(keep, blk, 0.0).astype(BF16)

    for j in range(RET_W // LANES):
        sl = slice(j * LANES, (j + 1) * LANES)
        rq = _rope_tile(proj(COL_RQ + j * LANES, LANES), cr_ref[:, sl], slr_ref[:, sl],
                        shr_ref[:, sl], DK_RET // 2)
        rq_ref[:, sl] = rq.astype(rq_ref.dtype)
        rk = _rope_tile(proj(COL_RK + j * LANES, LANES), cr_ref[:, sl], slr_ref[:, sl],
                        shr_ref[:, sl], DK_RET // 2) * (DK_RET ** -0.5)
        rk_ref[:, sl] = rk.astype(rk_ref.dtype)
    rv_ref[...] = proj(COL_RV, RET_W).astype(rv_ref.dtype)
    rg = proj(COL_RG, RET_W)
    rg_ref[...] = (rg * jax.nn.sigmoid(rg)).astype(rg_ref.dtype)
    ga_ref[...] = jax.nn.sigmoid(proj(COL_GA, D_MODEL)).astype(ga_ref.dtype)
    gb_ref[...] = jax.nn.sigmoid(proj(COL_GB, D_MODEL)).astype(gb_ref.dtype)


def _in_proj(x2d, tabs, tab_blocks, w, layer, tq, per_head):
    n = x2d.shape[0]
    grid = (n // tq,)
    row = lambda width: pl.BlockSpec((tq, width), lambda i: (i, 0))
    tab = lambda width: pl.BlockSpec((tq, width), lambda i: (i % tab_blocks, 0))
    heads = lambda width: pl.BlockSpec((N_HEADS, tq, width), lambda i: (0, i, 0))
    in_specs = [row(D_MODEL), _const_spec((1, D_MODEL), layer), _const_spec((D_MODEL, IN_COLS), layer),
                _const_spec((1, Q_LORA), layer)]
    args = [x2d, w["g_mix_pre"], w["w_in"], w["g_q"]]
    if per_head:
        in_specs += [_const_spec((Q_LORA, N_HEADS * LANES), layer), _const_spec((1, KV_LORA), layer),
                     _const_spec((N_HEADS, KV_LORA, LANES), layer),
                     _const_spec((N_HEADS * V_MLA, KV_LORA), layer)]
        args += [w["w_uq_h"], w["g_kv"], w["w_uk_h"], w["w_uv_t"]]
    else:
        in_specs += [_const_spec((Q_LORA, N_HEADS * (NOPE + ROPE)), layer), _const_spec((1, KV_LORA), layer),
                     _const_spec((N_HEADS, LANES, KV_LORA), layer)]
        args += [w["w_uq"], w["g_kv"], w["w_uk"]]
    in_specs += [tab(RET_W)] * 3 + [tab(N_HEADS * ROPE)] * 3 + ([tab(LANES)] * 3 if per_head else [])
    if per_head:
        out_shape = [jax.ShapeDtypeStruct((N_HEADS, n, LANES), BF16),
                     jax.ShapeDtypeStruct((N_HEADS, n, LANES), BF16),
                     jax.ShapeDtypeStruct((N_HEADS * V_MLA, n), BF16)]
        out_specs = [heads(LANES), heads(LANES), pl.BlockSpec((N_HEADS * V_MLA, tq), lambda i: (0, i))]
    else:
        out_shape = [jax.ShapeDtypeStruct((N_HEADS, n, KCAT), BF16)]
        out_specs = [heads(KCAT)]
    out_shape += [
        jax.ShapeDtypeStruct((n, KV_LORA), F32),
        jax.ShapeDtypeStruct((n, ROPE), F32),
        jax.ShapeDtypeStruct((n, RET_W), BF16),
        jax.ShapeDtypeStruct((n, RET_W), BF16),
        jax.ShapeDtypeStruct((n, RET_W), BF16),
        jax.ShapeDtypeStruct((n, RET_W), BF16),
        jax.ShapeDtypeStruct((n, D_MODEL), BF16),
        jax.ShapeDtypeStruct((n, D_MODEL), BF16),
    ]
    out_specs += [row(KV_LORA), row(ROPE), row(RET_W), row(RET_W), row(RET_W), row(RET_W),
                  row(D_MODEL), row(D_MODEL)]
    return pl.pallas_call(
        functools.partial(_in_proj_kernel, per_head=per_head),
        grid=grid, in_specs=in_specs, out_specs=out_specs, out_shape=out_shape,
        scratch_shapes=[pltpu.VMEM((tq, D_MODEL), BF16)],
        compiler_params=pltpu.CompilerParams(dimension_semantics=("parallel",),
                                             vmem_limit_bytes=VMEM_LIMIT_BYTES),
        name="in_proj",
    )(*args, *tabs)


def _attn_prompt_kernel(qi_ref, ki_ref, q_ref, k_ref, vt_ref, o_ref, m_scr, l_scr, acc_scr):
    step = pl.program_id(1)
    qi = qi_ref[step]
    ki = ki_ref[step]
    tq = q_ref.shape[1]

    @pl.when(ki == 0)
    def _():
        m_scr[...] = jnp.full(m_scr.shape, NEG_INF, F32)
        l_scr[...] = jnp.zeros(l_scr.shape, F32)
        acc_scr[...] = jnp.zeros(acc_scr.shape, F32)

    def update(masked):
        if masked:
            key = lax.broadcasted_iota(jnp.int32, (tq, tq), 0)
            qry = lax.broadcasted_iota(jnp.int32, (tq, tq), 1)
            causal = key <= qry
        for h in range(N_HEADS):
            st = _dot_nt(k_ref[h], q_ref[h])
            vt = vt_ref[h * V_MLA:(h + 1) * V_MLA, :]
            if masked:
                st = jnp.where(causal, st, NEG_INF)
            m_old = m_scr[h]
            m_new = jnp.maximum(m_old, jnp.max(st, axis=0, keepdims=True))
            alpha = jnp.exp2(m_old - m_new)
            p = jnp.exp2(st - m_new)
            l_scr[h] = alpha * l_scr[h] + jnp.sum(p, axis=0, keepdims=True)
            acc_scr[h] = alpha * acc_scr[h] + _dot(vt, p.astype(BF16))
            m_scr[h] = m_new

    @pl.when(ki < qi)
    def _():
        update(False)

    @pl.when(ki == qi)
    def _():
        update(True)

        o_t = jnp.concatenate([acc_scr[h] * (1.0 / l_scr[h]) for h in range(N_HEADS)], axis=0)
        o_ref[...] = o_t.T.astype(o_ref.dtype)


def _attn_prompt(q_heads, k_heads, v_t, batch, seq, tq):
    nq = seq // tq
    pairs = [(qi, ki) for qi in range(nq) for ki in range(qi + 1)]
    qi_tab = jnp.asarray([p[0] for p in pairs], jnp.int32)
    ki_tab = jnp.asarray([p[1] for p in pairs], jnp.int32)
    n = batch * seq
    grid_spec = pltpu.PrefetchScalarGridSpec(
        num_scalar_prefetch=2,
        grid=(batch, len(pairs)),
        in_specs=[
            pl.BlockSpec((N_HEADS, tq, LANES), lambda b, s, qt, kt: (0, b * nq + qt[s], 0)),
            pl.BlockSpec((N_HEADS, tq, LANES), lambda b, s, qt, kt: (0, b * nq + kt[s], 0)),
            pl.BlockSpec((N_HEADS * V_MLA, tq), lambda b, s, qt, kt: (0, b * nq + kt[s])),
        ],
        out_specs=pl.BlockSpec((tq, N_HEADS * V_MLA), lambda b, s, qt, kt: (b * nq + qt[s], 0)),
        scratch_shapes=[pltpu.VMEM((N_HEADS, 1, tq), F32), pltpu.VMEM((N_HEADS, 1, tq), F32),
                        pltpu.VMEM((N_HEADS, V_MLA, tq), F32)],
    )
    return pl.pallas_call(
        _attn_prompt_kernel, grid_spec=grid_spec,
        out_shape=jax.ShapeDtypeStruct((n, N_HEADS * V_MLA), BF16),
        compiler_params=pltpu.CompilerParams(dimension_semantics=("parallel", "arbitrary"),
                                             vmem_limit_bytes=VMEM_LIMIT_BYTES),
        name="attn_prompt",
    )(qi_tab, ki_tab, q_heads, k_heads, v_t)


def _attn_sample_kernel(pt_ref, ql_ref, qp_ref, kn_c_ref, kn_p_ref, wuv_ref, ckv_hbm, kpe_hbm, o_ref,
                        ckv_buf, kpe_buf, sem, *, layer, n_pages, chunk_pages, block_pages):
    b = pl.program_id(0)
    n_chunks = n_pages // chunk_pages
    ql = ql_ref[0]
    qp = qp_ref[0]
    t_new = ql.shape[0] // N_HEADS

    def chunk_copies(seq, j, slot):
        copies = []
        for i in range(chunk_pages):
            page = pt_ref[seq * n_pages + j * chunk_pages + i]
            copies.append(pltpu.make_async_copy(ckv_hbm.at[layer, page], ckv_buf.at[slot, i], sem.at[slot]))
            copies.append(pltpu.make_async_copy(kpe_hbm.at[layer, page], kpe_buf.at[slot, i], sem.at[slot]))
        return copies

    def start(copies):
        for c in copies:
            c.start()

    @pl.when(b == 0)
    def _():
        start(chunk_copies(b, 0, 0))

    def partial_softmax(s, v):
        m = jnp.max(s, axis=-1, keepdims=True)
        p = jnp.exp2(s - m)
        l = jnp.sum(p, axis=-1, keepdims=True)
        pb = p.astype(BF16)
        half = s.shape[1] // 2
        return [(m, l, _dot(pb[:, :half], v[:half])), (m, 0.0, _dot(pb[:, half:], v[half:]))]

    def merge(state, parts):
        m_new = functools.reduce(jnp.maximum, [m for m, _, _ in state + parts])
        l_new, acc_new = 0.0, 0.0
        for m, l, acc in state + parts:
            w = jnp.exp2(m - m_new)
            l_new = l_new + w * l
            acc_new = acc_new + w * acc
        return [(m_new, l_new, acc_new)]

    kc = kn_c_ref[0]
    s = _dot_nt(ql, kc) + _dot_nt(qp, kn_p_ref[0])
    r = lax.broadcasted_iota(jnp.int32, s.shape, 0) % t_new
    c = lax.broadcasted_iota(jnp.int32, s.shape, 1)
    state = merge([], partial_softmax(jnp.where(c <= r, s, NEG_INF), kc))

    for j in range(n_chunks):
        slot = j % 2
        if j + 1 < n_chunks:
            start(chunk_copies(b, j + 1, 1 - slot))
        else:
            @pl.when(b + 1 < pl.num_programs(0))
            def _():
                start(chunk_copies(b + 1, 0, 1 - slot))
        for cp in chunk_copies(b, j, slot):
            cp.wait()
        parts = []
        for u in range(0, chunk_pages, block_pages):
            kc = ckv_buf[slot, u:u + block_pages].reshape(block_pages * PAGE_SIZE, KV_LORA).astype(BF16)
            kp = jnp.concatenate([kpe_buf[slot, u + i] for i in range(block_pages)], axis=1).astype(BF16)
            parts += partial_softmax(_dot_nt(ql, kc) + _dot(qp, kp), kc)
        state = merge(state, parts)

    (_, l_fin, acc_fin), = state
    o = (acc_fin * (1.0 / l_fin)).astype(BF16)
    for p in range(N_PAIRS):
        even = _dot(o, wuv_ref[2 * p])[(2 * p) * t_new:(2 * p + 1) * t_new]
        odd = _dot(o, wuv_ref[2 * p + 1])[(2 * p + 1) * t_new:(2 * p + 2) * t_new]
        o_ref[0, :, p * LANES:(p + 1) * LANES] = even + odd


def _attn_sample(page_table, q_lat, q_pe, kn_c, kn_p, wuv, cache_ckv, cache_kpe_t, layer, chunk_pages,
                 block_pages):
    bdec, rows, _ = q_lat.shape
    n_pages = page_table.shape[1]
    assert n_pages % (2 * chunk_pages) == 0 and chunk_pages % block_pages == 0
    t_new = rows // N_HEADS
    pt_flat = page_table.reshape(-1)
    per_seq = lambda r, width: pl.BlockSpec((1, r, width), lambda b, pt: (b, 0, 0))
    grid_spec = pltpu.PrefetchScalarGridSpec(
        num_scalar_prefetch=1,
        grid=(bdec,),
        in_specs=[per_seq(rows, KV_LORA), per_seq(rows, ROPE),
                  per_seq(PAGE_SIZE, KV_LORA), per_seq(PAGE_SIZE, ROPE),
                  pl.BlockSpec((None, N_HEADS, KV_LORA, LANES), lambda b, pt: (layer, 0, 0, 0),
                               pipeline_mode=pl.Buffered(1)),
                  pl.BlockSpec(memory_space=pl.ANY), pl.BlockSpec(memory_space=pl.ANY)],
        out_specs=pl.BlockSpec((1, t_new, N_HEADS * V_MLA), lambda b, pt: (b, 0, 0)),
        scratch_shapes=[pltpu.VMEM((2, chunk_pages, PAGE_SIZE, KV_LORA), F32),
                        pltpu.VMEM((2, chunk_pages, ROPE, PAGE_SIZE), F32),
                        pltpu.SemaphoreType.DMA((2,))],
    )
    return pl.pallas_call(
        functools.partial(_attn_sample_kernel, layer=layer, n_pages=n_pages, chunk_pages=chunk_pages,
                          block_pages=block_pages),
        grid_spec=grid_spec,
        out_shape=jax.ShapeDtypeStruct((bdec, t_new, N_HEADS * V_MLA), F32),
        compiler_params=pltpu.CompilerParams(dimension_semantics=("arbitrary",),
                                             vmem_limit_bytes=VMEM_LIMIT_BYTES),
        name="attn_sample",
    )(pt_flat, q_lat, q_pe, kn_c, kn_p, wuv, cache_ckv, cache_kpe_t)


def _ret_intra_pair(q, k, v, d_even, d_odd, left):
    zero = jnp.zeros_like(q)
    s0 = _dot_nt(jnp.where(left, q, zero), k)
    s1 = _dot_nt(jnp.where(left, zero, q), k)
    p0 = (s0 * d_even).astype(BF16)
    p1 = (s1 * d_odd).astype(BF16)
    return _dot(p0, jnp.where(left, v, zero)) + _dot(p1, jnp.where(left, zero, v))


def _group_norm_gate(o, avg_ref, gate):
    mu = _dot(o.astype(BF16), avg_ref[...])
    d = o - mu
    var = _dot((d * d).astype(BF16), avg_ref[...])
    return d * lax.rsqrt(var + GN_EPS) * gate


def _ret_prompt_kernel(q_ref, k_ref, v_ref, gate_ref, dec_ref, qd_ref, kd_ref, gc_ref, avg_ref,
                       o_ref, s_out_ref, s_scr):
    c = pl.program_id(1)

    @pl.when(c == 0)
    def _():
        s_scr[...] = jnp.zeros(s_scr.shape, F32)

    lane = lax.broadcasted_iota(jnp.int32, (1, LANES), 1)
    left = lane < DK_RET
    rr = lax.broadcasted_iota(jnp.int32, (LANES, LANES), 0)
    cc = lax.broadcasted_iota(jnp.int32, (LANES, LANES), 1)
    same_head = (rr < DK_RET) == (cc < DV_RET)
    outs = []
    for p in range(N_PAIRS):
        sl = slice(p * LANES, (p + 1) * LANES)
        q = q_ref[:, sl]
        k = k_ref[:, sl]
        v = v_ref[:, sl]
        inner = _ret_intra_pair(q, k, v, dec_ref[2 * p], dec_ref[2 * p + 1], left)
        state = s_scr[p]
        cross = _dot(q, state.astype(BF16)) * qd_ref[p]
        outs.append(inner + cross)
        kd = (k.astype(F32) * kd_ref[p]).astype(BF16)
        upd = _dot_tn(kd, v)
        s_scr[p] = gc_ref[p] * state + jnp.where(same_head, upd, 0.0)
    o = jnp.concatenate(outs, axis=-1)
    o_ref[...] = _group_norm_gate(o, avg_ref, gate_ref[...].astype(F32)).astype(o_ref.dtype)

    @pl.when(c == pl.num_programs(1) - 1)
    def _():
        s_out_ref[0] = s_scr[...]


def _ret_prompt(rq, rk, rv, gate, tabs, avg, batch, seq, chunk):
    nc = seq // chunk
    n = batch * seq
    row = pl.BlockSpec((chunk, RET_W), lambda b, c: (b * nc + c, 0))
    dec, qd, kd, gc = tabs
    return pl.pallas_call(
        _ret_prompt_kernel, grid=(batch, nc),
        in_specs=[row, row, row, row, _const_spec(dec.shape), _const_spec(qd.shape),
                  _const_spec(kd.shape), _const_spec(gc.shape), _const_spec(avg.shape)],
        out_specs=(row, pl.BlockSpec((1, N_PAIRS, LANES, LANES), lambda b, c: (b, 0, 0, 0))),
        out_shape=(jax.ShapeDtypeStruct((n, RET_W), BF16),
                   jax.ShapeDtypeStruct((batch, N_PAIRS, LANES, LANES), F32)),
        scratch_shapes=[pltpu.VMEM((N_PAIRS, LANES, LANES), F32)],
        compiler_params=pltpu.CompilerParams(dimension_semantics=("parallel", "arbitrary"),
                                             vmem_limit_bytes=VMEM_LIMIT_BYTES),
        name="ret_prompt",
    )(rq, rk, rv, gate, dec, qd, kd, gc, avg)


def _ret_sample_kernel(q_ref, k_ref, v_ref, gate_ref, s_ref, dec_ref, qd_ref, kd_ref, gc_ref,
                       seqmask_ref, avg_ref, o_ref, s_out_ref):
    nseq = s_ref.shape[0]
    lane = lax.broadcasted_iota(jnp.int32, (1, LANES), 1)
    left = lane < DK_RET
    seqmask = seqmask_ref[...]
    reps = nseq * DK_RET // LANES
    outs = []
    for p in range(N_PAIRS):
        sl = slice(p * LANES, (p + 1) * LANES)
        q = q_ref[:, sl]
        k = k_ref[:, sl]
        v = v_ref[:, sl]
        inner = _ret_intra_pair(q, k, v, dec_ref[2 * p], dec_ref[2 * p + 1], left)
        qf = q.astype(F32)
        kf = k.astype(F32) * kd_ref[p]
        q_sw = pltpu.roll(qf, DK_RET, 1)
        k_sw = pltpu.roll(kf, DK_RET, 1)
        cross = []
        for e in range(2):
            h = 2 * p + e
            keep = left if e == 0 else jnp.logical_not(left)
            qh = jnp.where(keep, qf, q_sw)
            kh = jnp.where(keep, kf, k_sw)
            q_exp = (jnp.concatenate([qh] * reps, axis=-1) * seqmask).astype(BF16)
            k_exp = (jnp.concatenate([kh] * reps, axis=-1) * seqmask).astype(BF16)
            state = s_ref[:, h].reshape(nseq * DK_RET, DV_RET)
            cross.append(_dot(q_exp, state.astype(BF16)) * qd_ref[h])
            vh = v[:, e * DV_RET:(e + 1) * DV_RET]
            upd = _dot_tn(k_exp, vh)
            s_out_ref[:, h] = (gc_ref[h] * state + upd).reshape(nseq, DK_RET, DV_RET)
        outs.append(inner + jnp.concatenate(cross, axis=-1))
    o = jnp.concatenate(outs, axis=-1)
    o_ref[...] = _group_norm_gate(o, avg_ref, gate_ref[...].astype(F32)).astype(o_ref.dtype)


def _ret_sample(rq, rk, rv, gate, state, layer, tabs, seqmask, avg, nseq, t_new):
    n = rq.shape[0]
    rows = nseq * t_new
    bdec = n // t_new
    row = pl.BlockSpec((rows, RET_W), lambda i: (i, 0))
    st_in = pl.BlockSpec((None, nseq, N_HEADS, DK_RET, DV_RET), lambda i: (layer, i, 0, 0, 0))
    st_out = pl.BlockSpec((nseq, N_HEADS, DK_RET, DV_RET), lambda i: (i, 0, 0, 0))
    dec, qd, kd, gc = tabs
    return pl.pallas_call(
        _ret_sample_kernel, grid=(bdec // nseq,),
        in_specs=[row, row, row, row, st_in, _const_spec(dec.shape), _const_spec(qd.shape),
                  _const_spec(kd.shape), _const_spec(gc.shape), _const_spec(seqmask.shape),
                  _const_spec(avg.shape)],
        out_specs=(row, st_out),
        out_shape=(jax.ShapeDtypeStruct((n, RET_W), BF16),
                   jax.ShapeDtypeStruct((bdec, N_HEADS, DK_RET, DV_RET), F32)),
        compiler_params=pltpu.CompilerParams(dimension_semantics=("parallel",),
                                             vmem_limit_bytes=VMEM_LIMIT_BYTES),
        name="ret_sample",
    )(rq, rk, rv, gate, state, dec, qd, kd, gc, seqmask, avg)


FFN_CHUNK = 256


def _merge_ffn_kernel(*refs, long_seq, tiles_per_seq, seq_rows):
    if long_seq:
        (x_ref, om_ref, orr_ref, ga_ref, gb_ref, wom_ref, wor_ref, wout_ref, gpost_ref, gfpre_ref,
         wup_ref, cw_ref, cb_ref, wdn_ref, gfpost_ref, y_ref, cnew_ref, a_scr, h_scr, xn_scr) = refs
    else:
        (x_ref, om_ref, orr_ref, ga_ref, gb_ref, wom_ref, wor_ref, wout_ref, gpost_ref, gfpre_ref,
         wup_ref, cw_ref, cb_ref, wdn_ref, gfpost_ref, e1_ref, e2_ref, y_ref, a_ref,
         a_scr, h_scr, xn_scr) = refs
    tq = x_ref.shape[0]
    a1 = _dot(om_ref[...].astype(BF16), wom_ref[...])
    a2 = _dot(orr_ref[...].astype(BF16), wor_ref[...])
    mixed = (ga_ref[...].astype(F32) * a1 + gb_ref[...].astype(F32) * a2).astype(BF16)
    x1 = x_ref[...] + _rms(_dot(mixed, wout_ref[...]), gpost_ref[...])
    xn_scr[...] = _rms(x1, gfpre_ref[...]).astype(BF16)

    @pl.when(pl.program_id(0) % tiles_per_seq == 0)
    def _():
        a_scr[0:SUBLANES, :] = jnp.zeros((SUBLANES, FFN_DIM), F32)

    if not long_seq:
        row_in_seq = lax.broadcasted_iota(jnp.int32, (tq, 1), 0) % seq_rows

    for c0 in range(0, FFN_DIM, FFN_CHUNK):
        sl = slice(c0, c0 + FFN_CHUNK)
        a = _dot(xn_scr[...], wup_ref[:, c0:c0 + FFN_CHUNK])
        b = _dot(xn_scr[...], wup_ref[:, FFN_DIM + c0:FFN_DIM + c0 + FFN_CHUNK])
        a_scr[SUBLANES:SUBLANES + tq, sl] = a
        a_m1 = a_scr[SUBLANES - 1:SUBLANES - 1 + tq, sl]
        a_m2 = a_scr[SUBLANES - 2:SUBLANES - 2 + tq, sl]
        if not long_seq:
            a_m1 = jnp.where(row_in_seq == 0, e1_ref[:, sl], a_m1)
            a_m2 = jnp.where(row_in_seq < 2, e2_ref[:, sl], a_m2)
            a_ref[:, sl] = a
        conv = cb_ref[:, sl] + a_m2 * cw_ref[0:1, sl] + a_m1 * cw_ref[1:2, sl] + a * cw_ref[2:3, sl]
        h_scr[:, sl] = (jax.nn.gelu(conv, approximate=True) * b).astype(BF16)
    f = _dot(h_scr[...], wdn_ref[...])
    y_ref[...] = x1 + _rms(f, gfpost_ref[...])
    if long_seq:
        cnew_ref[0] = a_scr[tq + SUBLANES - (CONV_W - 1):tq + SUBLANES, :]
        a_scr[0:SUBLANES, :] = a_scr[tq:tq + SUBLANES, :]


def _merge_ffn(x2d, o_mla, o_ret, ga, gb, w, layer, tq, seq, bounds=None):
    n = x2d.shape[0]
    long_seq = bounds is None
    assert (seq % tq == 0) if long_seq else (tq % seq == 0)
    row = lambda width: pl.BlockSpec((tq, width), lambda i: (i, 0))
    cs = lambda shape: _const_spec(shape, layer)
    in_specs = [row(D_MODEL), row(N_HEADS * V_MLA), row(RET_W), row(D_MODEL), row(D_MODEL),
                cs((N_HEADS * V_MLA, D_MODEL)), cs((RET_W, D_MODEL)),
                cs((D_MODEL, D_MODEL)), cs((1, D_MODEL)), cs((1, D_MODEL)),
                cs((D_MODEL, 2 * FFN_DIM)), cs((CONV_W, FFN_DIM)),
                cs((1, FFN_DIM)), cs((FFN_DIM, D_MODEL)), cs((1, D_MODEL))]
    args = [x2d, o_mla, o_ret, ga, gb, w["w_o_mla"], w["w_o_ret"], w["w_out"], w["g_mix_post"],
            w["g_ffn_pre"], w["w_up"], w["conv_w"], w["conv_b"], w["w_down"], w["g_ffn_post"]]
    if long_seq:
        tiles_per_seq = seq // tq
        out_shape = (jax.ShapeDtypeStruct((n, D_MODEL), F32),
                     jax.ShapeDtypeStruct((n // seq, CONV_W - 1, FFN_DIM), F32))
        out_specs = (row(D_MODEL),
                     pl.BlockSpec((1, CONV_W - 1, FFN_DIM), lambda i: (i // tiles_per_seq, 0, 0)))
    else:
        tiles_per_seq = n // tq
        in_specs += [row(FFN_DIM), row(FFN_DIM)]
        args += list(bounds)
        out_shape = (jax.ShapeDtypeStruct((n, D_MODEL), F32), jax.ShapeDtypeStruct((n, FFN_DIM), F32))
        out_specs = (row(D_MODEL), row(FFN_DIM))
    return pl.pallas_call(
        functools.partial(_merge_ffn_kernel, long_seq=long_seq, tiles_per_seq=tiles_per_seq,
                          seq_rows=seq),
        grid=(n // tq,), in_specs=in_specs, out_specs=out_specs, out_shape=out_shape,
        scratch_shapes=[pltpu.VMEM((tq + SUBLANES, FFN_DIM), F32), pltpu.VMEM((tq, FFN_DIM), BF16),
                        pltpu.VMEM((tq, D_MODEL), BF16)],
        compiler_params=pltpu.CompilerParams(dimension_semantics=("arbitrary",),
                                             vmem_limit_bytes=VMEM_LIMIT_BYTES),
        name="merge_ffn",
    )(*args)


def _rope_tables(pos, d, reps):
    half = d // 2
    inv = jnp.power(ROPE_THETA, -jnp.arange(half, dtype=F32) * (2.0 / d))
    ang = pos.astype(F32)[:, None] * inv[None, :]
    cos, sin = jnp.cos(ang), jnp.sin(ang)
    zero = jnp.zeros_like(sin)
    rep = lambda a: jnp.tile(a, (1, reps))
    return (rep(jnp.concatenate([cos, cos], -1)), rep(jnp.concatenate([-sin, zero], -1)),
            rep(jnp.concatenate([zero, sin], -1)))


def _head_tile_rope_tables(pos):
    cos, sin_lo, sin_hi = _rope_tables(pos, ROPE, 1)
    n = pos.shape[0]
    place = lambda a, fill: jnp.concatenate(
        [jnp.full((n, NOPE), fill, F32), a, jnp.zeros((n, LANES - NOPE - ROPE), F32)], axis=-1)
    return place(cos, 1.0), place(sin_lo, 0.0), place(sin_hi, 0.0)


def _log_decay():
    return jnp.log1p(-jnp.exp2(-5.0 - jnp.arange(N_HEADS, dtype=F32)))


def _pair_lanes(per_head):
    c = per_head.shape[1]
    x = jnp.broadcast_to(per_head[:, :, None], (N_HEADS, c, DK_RET))
    return x.reshape(N_PAIRS, 2, c, DK_RET).transpose(0, 2, 1, 3).reshape(N_PAIRS, c, LANES)


def _ret_tables_prompt(chunk):
    log_g = _log_decay()
    idx = jnp.arange(chunk, dtype=F32)
    diff = idx[:, None] - idx[None, :]
    dec = jnp.where(diff[None] >= 0, jnp.exp(jnp.maximum(diff, 0.0)[None] * log_g[:, None, None]), 0.0)
    qd = _pair_lanes(jnp.exp((idx + 1.0)[None, :] * log_g[:, None]))
    kd = _pair_lanes(jnp.exp((chunk - 1.0 - idx)[None, :] * log_g[:, None]))
    gc = _pair_lanes(jnp.exp(chunk * log_g)[:, None])
    return dec, qd, kd, gc


def _ret_tables_sample(nseq, t_new):
    log_g = _log_decay()
    rows = nseq * t_new
    r = jnp.arange(rows)
    t = (r % t_new).astype(F32)
    same = (r[:, None] // t_new) == (r[None, :] // t_new)
    diff = t[:, None] - t[None, :]
    dec = jnp.where((same & (diff >= 0))[None],
                    jnp.exp(jnp.maximum(diff, 0.0)[None] * log_g[:, None, None]), 0.0)
    qd = jnp.exp((t + 1.0)[None, :] * log_g[:, None])[:, :, None]
    kd = _pair_lanes(jnp.exp((t_new - 1.0 - t)[None, :] * log_g[:, None]))
    gc = jnp.exp(t_new * log_g)[:, None, None]
    seqmask = ((r[:, None] // t_new) == (jnp.arange(nseq * DK_RET)[None, :] // DK_RET)).astype(F32)
    return (dec, qd, kd, gc), seqmask


def _group_avg():
    g = jnp.arange(RET_W) // DV_RET
    return jnp.where(g[:, None] == g[None, :], 1.0 / DV_RET, 0.0).astype(BF16)


def _prep_weights(norm_mix_pre, norm_mix_post, norm_ffn_pre, norm_ffn_post, w_in, norm_q_a, w_uq,
                  norm_kv_a, w_uk, w_uv, w_o_mla, w_o_ret, w_out, w_up, conv_w, conv_b, w_down):
    depth = w_in.shape[0]
    k0 = Q_LORA + KV_LORA
    w_in_r = jnp.concatenate([w_in[:, :, :k0], w_in[:, :, k0 + ROPE:],
                              jnp.tile(w_in[:, :, k0:k0 + ROPE], (1, 1, LANES // ROPE))], axis=-1)
    uq = w_uq.reshape(depth, Q_LORA, N_HEADS, NOPE + ROPE)
    w_uq_r = jnp.concatenate([uq[..., :NOPE].reshape(depth, Q_LORA, N_HEADS * NOPE),
                              uq[..., NOPE:].reshape(depth, Q_LORA, N_HEADS * ROPE)], axis=-1)
    uk = jnp.transpose(w_uk, (0, 2, 3, 1))
    odd = (jnp.arange(N_HEADS) % 2 == 1)[None, :, None, None]
    zeros = jnp.zeros_like(uk)
    w_uk_r = jnp.concatenate([jnp.where(odd, zeros, uk), jnp.where(odd, uk, zeros)], axis=2)
    uv = jnp.transpose(w_uv, (0, 2, 1, 3))
    zeros = jnp.zeros_like(uv)
    w_uv_r = jnp.concatenate([jnp.where(odd, zeros, uv), jnp.where(odd, uv, zeros)], axis=3)
    pad = jnp.zeros((depth, Q_LORA, N_HEADS, LANES - NOPE - ROPE), w_uq.dtype)
    w_uq_h = jnp.concatenate([uq, pad], axis=-1).reshape(depth, Q_LORA, N_HEADS * LANES)
    uk_h = jnp.transpose(w_uk, (0, 2, 1, 3))
    w_uk_h = jnp.concatenate([uk_h, jnp.zeros(uk_h.shape[:3] + (LANES - NOPE,), w_uk.dtype)], axis=-1)
    w_uv_t = jnp.swapaxes(w_uv.reshape(depth, KV_LORA, N_HEADS * V_MLA), 1, 2)
    vec = lambda g: g[:, None, :]
    return dict(
        w_uq_h=w_uq_h.astype(BF16), w_uk_h=w_uk_h.astype(BF16), w_uv_t=w_uv_t.astype(BF16),
        g_mix_pre=vec(norm_mix_pre), g_mix_post=vec(norm_mix_post), g_ffn_pre=vec(norm_ffn_pre),
        g_ffn_post=vec(norm_ffn_post), g_q=vec(norm_q_a), g_kv=vec(norm_kv_a),
        w_in=w_in_r.astype(BF16), w_uq=w_uq_r.astype(BF16), w_uk=w_uk_r.astype(BF16),
        w_uv=w_uv_r.astype(BF16), w_o_mla=w_o_mla.astype(BF16), w_o_ret=w_o_ret.astype(BF16),
        w_out=w_out.astype(BF16), w_up=w_up.astype(BF16), conv_w=conv_w, conv_b=vec(conv_b),
        w_down=w_down.astype(BF16))


def _pair_state_to_heads(sp):
    even = sp[:, :, :DK_RET, :DV_RET]
    odd = sp[:, :, DK_RET:, DV_RET:]
    return jnp.stack([even, odd], axis=2).reshape(sp.shape[0], N_HEADS, DK_RET, DV_RET)


def _pick_tile(n, target):
    t = min(n, target)
    while n % t:
        t //= 2
    return t


def kernel(x_prompt, x_sample, cache_ckv, cache_kpe, state_ret, state_conv, page_table, norm_mix_pre, norm_mix_post, norm_ffn_pre, norm_ffn_post, w_in, norm_q_a, w_uq, norm_kv_a, w_uk, w_uv, w_o_mla, w_o_ret, w_out, w_up, conv_w, conv_b, w_down):
    batch, seq, _ = x_prompt.shape
    bdec, t_new, _ = x_sample.shape
    depth = w_in.shape[0]
    n_pages = page_table.shape[1]
    past_len = n_pages * PAGE_SIZE
    n_p, n_s = batch * seq, bdec * t_new

    tq_p = _pick_tile(seq, 256)
    tq_s = _pick_tile(n_s, 256)
    chunk = _pick_tile(seq, 256)
    ret_nseq = _pick_tile(bdec, LANES // t_new)
    tq_attn = _pick_tile(seq, 512)
    chunk_pages = _pick_tile(n_pages // 2, 32)
    block_pages = chunk_pages
    cache_kpe_t = jnp.swapaxes(cache_kpe, 2, 3)

    weights = _prep_weights(norm_mix_pre, norm_mix_post, norm_ffn_pre, norm_ffn_post, w_in, norm_q_a,
                            w_uq, norm_kv_a, w_uk, w_uv, w_o_mla, w_o_ret, w_out, w_up, conv_w, conv_b,
                            w_down)
    pos_p = jnp.arange(seq, dtype=jnp.int32)
    pos_s = past_len + (jnp.arange(tq_s, dtype=jnp.int32) % t_new)
    tabs_p = (_rope_tables(pos_p, DK_RET, N_HEADS) + _rope_tables(pos_p, ROPE, N_HEADS)
              + _head_tile_rope_tables(pos_p))
    tabs_s = _rope_tables(pos_s, DK_RET, N_HEADS) + _rope_tables(pos_s, ROPE, N_HEADS)
    ret_tabs_p = _ret_tables_prompt(chunk)
    ret_tabs_s, seqmask = _ret_tables_sample(ret_nseq, t_new)
    avg = _group_avg()

    xp = x_prompt.reshape(n_p, D_MODEL)
    xs = x_sample.reshape(n_s, D_MODEL)
    outs = [[] for _ in range(8)]
    w = weights
    for l in range(depth):
        (q_heads, k_heads, v_t, ckv, kpe, rq, rk, rv, rg, ga, gb) = _in_proj(
            xp, tabs_p, seq // tq_p, w, l, tq_p, per_head=True)
        o_mla = _attn_prompt(q_heads, k_heads, v_t, batch, seq, tq_attn)
        o_ret, s_pair = _ret_prompt(rq, rk, rv, rg, ret_tabs_p, avg, batch, seq, chunk)
        xp, conv_new = _merge_ffn(xp, o_mla, o_ret, ga, gb, w, l, tq_p, seq)
        outs[0].append(ckv.reshape(batch, seq, KV_LORA))
        outs[1].append(kpe.reshape(batch, seq, ROPE))
        outs[2].append(_pair_state_to_heads(s_pair))
        outs[3].append(conv_new)

        (qcat, ckv, kpe, rq, rk, rv, rg, ga, gb) = _in_proj(xs, tabs_s, 1, w, l, tq_s, per_head=False)
        q4 = qcat.reshape(N_HEADS, bdec, t_new, KCAT).transpose(1, 0, 2, 3)
        q_lat = q4[..., :KV_LORA].reshape(bdec, N_HEADS * t_new, KV_LORA)
        q_pe = q4[..., KV_LORA:].astype(F32).reshape(bdec, N_HEADS, t_new, LANES // ROPE, ROPE).sum(axis=3)
        q_pe = q_pe.astype(BF16).reshape(bdec, N_HEADS * t_new, ROPE)
        pad = ((0, 0), (0, PAGE_SIZE - t_new), (0, 0))
        kn_c = jnp.pad(ckv.reshape(bdec, t_new, KV_LORA).astype(BF16), pad)
        kn_p = jnp.pad(kpe.reshape(bdec, t_new, ROPE).astype(BF16), pad)
        o_mla = _attn_sample(page_table, q_lat, q_pe, kn_c, kn_p, w["w_uv"], cache_ckv, cache_kpe_t, l,
                             chunk_pages, block_pages).reshape(n_s, N_HEADS * V_MLA)
        o_ret, s_new = _ret_sample(rq, rk, rv, rg, state_ret, l, ret_tabs_s, seqmask, avg, ret_nseq, t_new)
        buf = state_conv[l]
        zrow = jnp.zeros((bdec, t_new - 1, FFN_DIM), F32)
        e1 = jnp.concatenate([buf[:, 1:2], zrow], axis=1).reshape(n_s, FFN_DIM)
        e2 = jnp.concatenate([buf, zrow[:, 1:]], axis=1).reshape(n_s, FFN_DIM)
        xs, a_full = _merge_ffn(xs, o_mla, o_ret, ga, gb, w, l, tq_s, t_new, bounds=(e1, e2))
        outs[4].append(ckv.reshape(bdec, t_new, KV_LORA))
        outs[5].append(kpe.reshape(bdec, t_new, ROPE))
        outs[6].append(s_new)
        outs[7].append(a_full.reshape(bdec, t_new, FFN_DIM)[:, t_new - (CONV_W - 1):])

    stacked = [jnp.stack(o) for o in outs]
    return (xp.reshape(batch, seq, D_MODEL), xs.reshape(bdec, t_new, D_MODEL), *stacked)
```
